```python
import math
import jax, jax.numpy as jnp
from jax import lax
import numpy as np

D_MODEL = 2048
BATCH = 4
SEQ = 4096
DEPTH = 2

CHUNK = 64
Q_BLOCK = 128
MLA_HEADS = 8
MLA_Q_LORA = 512
MLA_KV_LORA = 256
MLA_NOPE_DIM = 128
MLA_ROPE_DIM = 64
MLA_V_DIM = 128
RET_HEADS = 4
RET_QK_DIM = 256
RET_V_DIM = 256
MIX_WIDTH = MLA_HEADS * MLA_V_DIM + RET_HEADS * RET_V_DIM
D_FF = -(-(8 * D_MODEL) // (3 * 256)) * 256
ROPE_THETA = 10000.0
LN_EPS = 1e-5
RMS_EPS = 1e-6
GN_EPS = 1e-5
ALPHA = (2 * DEPTH) ** 0.25
BETA = (8 * DEPTH) ** -0.25
IN_SIZES = (MLA_Q_LORA, MLA_KV_LORA, MLA_ROPE_DIM,
            RET_HEADS * RET_QK_DIM, RET_HEADS * RET_QK_DIM,
            RET_HEADS * RET_V_DIM, RET_HEADS * RET_V_DIM)
D_IN = sum(IN_SIZES)

kernel_name = "hybrid_mla_retention_deepnorm"


def layer_norm(x, g, b):
    xf = x.astype(jnp.float32)
    mu = xf.mean(-1, keepdims=True)
    var = jnp.square(xf - mu).mean(-1, keepdims=True)
    return ((xf - mu) * lax.rsqrt(var + LN_EPS) * g + b).astype(x.dtype)


def rms_norm(x, g):
    xf = x.astype(jnp.float32)
    return (xf * lax.rsqrt(jnp.square(xf).mean(-1, keepdims=True) + RMS_EPS) * g).astype(x.dtype)


def rope_tables(positions, dim):
    inv_freq = ROPE_THETA ** (-jnp.arange(0, dim, 2, dtype=jnp.float32) / dim)
    ang = positions.astype(jnp.float32)[..., None] * inv_freq
    return jnp.cos(ang), jnp.sin(ang)


def apply_rope(t, cos, sin):
    tf = t.astype(jnp.float32)
    half = t.shape[-1] // 2
    t1, t2 = tf[..., :half], tf[..., half:]
    c, s = cos[:, :, None, :], sin[:, :, None, :]
    return jnp.concatenate([t1 * c - t2 * s, t2 * c + t1 * s], axis=-1).astype(t.dtype)


def split_columns(h):
    parts, start = [], 0
    for size in IN_SIZES:
        parts.append(h[..., start:start + size])
        start += size
    return parts


def mla_group(c_q, c_kv, k_rope, cos, sin, q_norm_g, kv_norm_g, w_uq, w_ukv):
    B, S, _ = c_q.shape
    H = MLA_HEADS
    q = (rms_norm(c_q, q_norm_g) @ w_uq).reshape(B, S, H, MLA_NOPE_DIM + MLA_ROPE_DIM)
    q_nope = q[..., :MLA_NOPE_DIM]
    q_rope = apply_rope(q[..., MLA_NOPE_DIM:], cos, sin)
    kv = (rms_norm(c_kv, kv_norm_g) @ w_ukv).reshape(B, S, H, MLA_NOPE_DIM + MLA_V_DIM)
    k_nope, v = kv[..., :MLA_NOPE_DIM], kv[..., MLA_NOPE_DIM:]
    k_r = apply_rope(k_rope[:, :, None, :], cos, sin)[:, :, 0, :]
    scale = (MLA_NOPE_DIM + MLA_ROPE_DIM) ** -0.5
    chunk_id = jnp.arange(S) // CHUNK
    neg = jnp.finfo(jnp.float32).min
    outs = []
    for blk in range(S // Q_BLOCK):
        q0 = blk * Q_BLOCK
        kend = q0 + Q_BLOCK
        s = (jnp.einsum('bqhd,bkhd->bhqk', q_nope[:, q0:kend], k_nope[:, :kend])
             + jnp.einsum('bqhr,bkr->bhqk', q_rope[:, q0:kend], k_r[:, :kend]))
        s = s.astype(jnp.float32) * scale
        mask = chunk_id[q0:kend, None] >= chunk_id[None, :kend]
        s = jnp.where(mask[None, None], s, neg)
        p = jax.nn.softmax(s, axis=-1).astype(v.dtype)
        outs.append(jnp.einsum('bhqk,bkhd->bqhd', p, v[:, :kend]))
    o = jnp.concatenate(outs, axis=1)
    return o.reshape(B, S, H * MLA_V_DIM)


def retention_group(rq, rk, rv, rg, cos, sin, gn_g, gn_b):
    B, S, _ = rq.shape
    H, DK, DV, L = RET_HEADS, RET_QK_DIM, RET_V_DIM, CHUNK
    NC = S // L
    f32 = jnp.float32
    q = apply_rope(rq.reshape(B, S, H, DK), cos, sin).astype(f32) * (DK ** -0.5)
    k = apply_rope(rk.reshape(B, S, H, DK), cos, sin).astype(f32)
    v = rv.reshape(B, S, H, DV).astype(f32)
    q = q.reshape(B, NC, L, H, DK)
    k = k.reshape(B, NC, L, H, DK)
    v = v.reshape(B, NC, L, H, DV)
    log_gamma = jnp.log1p(-jnp.exp2(-5.0 - jnp.arange(H, dtype=f32)))
    idx = jnp.arange(L, dtype=f32)
    intra_decay = jnp.exp(log_gamma[:, None, None] * jnp.abs(idx[:, None] - idx[None, :]))
    scores = jnp.einsum('bcnhd,bcmhd->bchnm', q, k) * intra_decay[None, None]
    o_intra = jnp.einsum('bchnm,bcmhe->bcnhe', scores, v)
    q_decay = jnp.exp(log_gamma[:, None] * (idx + 1.0))[None]
    k_decay = jnp.exp(log_gamma[:, None] * (L - 1.0 - idx))
    chunk_decay = jnp.exp(log_gamma * L)
    q_decay = q_decay[0]

    def step(state, inp):
        qc, kc, vc = inp
        o_inter = jnp.einsum('bnhd,hn,bhde->bnhe', qc, q_decay, state)
        state = (state * chunk_decay[None, :, None, None]
                 + jnp.einsum('bmhd,hm,bmhe->bhde', kc, k_decay, vc))
        return state, o_inter

    state0 = jnp.zeros((B, H, DK, DV), f32)
    xs = (q.transpose(1, 0, 2, 3, 4), k.transpose(1, 0, 2, 3, 4), v.transpose(1, 0, 2, 3, 4))
    _, o_inter = lax.scan(step, state0, xs)
    o = (o_intra + o_inter.transpose(1, 0, 2, 3, 4)).reshape(B, S, H, DV)
    mu = o.mean(-1, keepdims=True)
    var = jnp.square(o - mu).mean(-1, keepdims=True)
    o = ((o - mu) * lax.rsqrt(var + GN_EPS)).reshape(B, S, H * DV) * gn_g + gn_b
    o = jax.nn.silu(rg.astype(f32)) * o
    return o.astype(rq.dtype)


def setup_inputs(seed: int = 0) -> dict:
    key = jax.random.key(seed)
    ks = list(jax.random.split(key, 24))
    f32 = jnp.float32

    def nrm(k, shape, scale):
        return jax.random.normal(k, shape, f32) * scale

    x = jax.random.normal(ks[0], (BATCH, SEQ, D_MODEL), f32)
    start = jax.random.randint(ks[1], (BATCH, 1), 0, 4096, dtype=jnp.int32)
    positions = (start + jnp.arange(SEQ, dtype=jnp.int32)[None, :]).astype(jnp.int32)
    return {
        "x": x,
        "positions": positions,
        "ln_in_g": 1.0 + nrm(ks[2], (D_MODEL,), 0.02),
        "ln_in_b": nrm(ks[3], (D_MODEL,), 0.02),
        "w_in": nrm(ks[4], (DEPTH, D_MODEL, D_IN), D_MODEL ** -0.5),
        "q_norm_g": 1.0 + nrm(ks[5], (DEPTH, MLA_Q_LORA), 0.02),
        "kv_norm_g": 1.0 + nrm(ks[6], (DEPTH, MLA_KV_LORA), 0.02),
        "w_uq": nrm(ks[7], (DEPTH, MLA_Q_LORA, MLA_HEADS * (MLA_NOPE_DIM + MLA_ROPE_DIM)), MLA_Q_LORA ** -0.5),
        "w_ukv": nrm(ks[8], (DEPTH, MLA_KV_LORA, MLA_HEADS * (MLA_NOPE_DIM + MLA_V_DIM)), MLA_KV_LORA ** -0.5),
        "ret_gn_g": 1.0 + nrm(ks[9], (DEPTH, RET_HEADS * RET_V_DIM), 0.02),
        "ret_gn_b": nrm(ks[10], (DEPTH, RET_HEADS * RET_V_DIM), 0.02),
        "w_out": nrm(ks[11], (DEPTH, MIX_WIDTH, D_MODEL), (MIX_WIDTH ** -0.5) * BETA),
        "ln1_g": 1.0 + nrm(ks[12], (DEPTH, D_MODEL), 0.02),
        "ln1_b": nrm(ks[13], (DEPTH, D_MODEL), 0.02),
        "w_gate": nrm(ks[14], (DEPTH, D_MODEL, D_FF), D_MODEL ** -0.5),
        "w_up": nrm(ks[15], (DEPTH, D_MODEL, D_FF), D_MODEL ** -0.5),
        "w_down": nrm(ks[16], (DEPTH, D_FF, D_MODEL), (D_FF ** -0.5) * BETA),
        "ln2_g": 1.0 + nrm(ks[17], (DEPTH, D_MODEL), 0.02),
        "ln2_b": nrm(ks[18], (DEPTH, D_MODEL), 0.02),
    }


def reference(x, positions, ln_in_g, ln_in_b, w_in, q_norm_g, kv_norm_g, w_uq, w_ukv,
              ret_gn_g, ret_gn_b, w_out, ln1_g, ln1_b, w_gate, w_up, w_down, ln2_g, ln2_b):
    cos_m, sin_m = rope_tables(positions, MLA_ROPE_DIM)
    cos_r, sin_r = rope_tables(positions, RET_QK_DIM)
    x = layer_norm(x, ln_in_g, ln_in_b)
    for l in range(DEPTH):
        h = x @ w_in[l]
        c_q, c_kv, k_rope, rq, rk, rv, rg = split_columns(h)
        a = mla_group(c_q, c_kv, k_rope, cos_m, sin_m, q_norm_g[l], kv_norm_g[l], w_uq[l], w_ukv[l])
        r = retention_group(rq, rk, rv, rg, cos_r, sin_r, ret_gn_g[l], ret_gn_b[l])
        mix = jnp.concatenate([a, r], axis=-1) @ w_out[l]
        x = layer_norm(ALPHA * x + mix, ln1_g[l], ln1_b[l])
        f = (jax.nn.silu(x @ w_gate[l]) * (x @ w_up[l])) @ w_down[l]
        x = layer_norm(ALPHA * x + f, ln2_g[l], ln2_b[l])
    return x
```

```python
import functools

import jax
import jax.numpy as jnp
from jax import lax
from jax.experimental import pallas as pl
from jax.experimental.pallas import tpu as pltpu

F32 = jnp.float32
BF16 = jnp.bfloat16

D_MODEL = 2048
CHUNK = 64
MLA_HEADS = 8
MLA_Q_LORA = 512
MLA_KV_LORA = 256
MLA_NOPE_DIM = 128
MLA_ROPE_DIM = 64
MLA_V_DIM = 128
MLA_QK_DIM = MLA_NOPE_DIM + MLA_ROPE_DIM
RET_HEADS = 4
RET_QK_DIM = 256
RET_V_DIM = 256
ROPE_THETA = 10000.0
LN_EPS = 1e-5
RMS_EPS = 1e-6
GN_EPS = 1e-5

COL_CQ = 0
COL_CKV = 512
COL_KR = 768
COL_RQ = 1024
COL_RK = 2048
COL_RV = 3072
COL_RG = 4096
D_IN_PAD = 5120

VMEM_LIMIT = 56 * 1024 * 1024

MASK_VALUE = -1e30


def _params(*sem):
    return pltpu.CompilerParams(dimension_semantics=sem, vmem_limit_bytes=VMEM_LIMIT)


def _layer_norm(y, g, b):
    mu = jnp.mean(y, axis=-1, keepdims=True)
    d = y - mu
    var = jnp.mean(d * d, axis=-1, keepdims=True)
    return d * lax.rsqrt(var + LN_EPS) * g + b


def _ln_in_kernel(x_ref, g_ref, b_ref, of_ref, ob_ref):
    y = _layer_norm(x_ref[...], g_ref[...], b_ref[...])
    of_ref[...] = y
    ob_ref[...] = y.astype(BF16)


def ln_in(x, g, b, bm=512):
    n, d = x.shape
    row = pl.BlockSpec((bm, d), lambda i: (i, 0))
    vec = pl.BlockSpec((1, d), lambda i: (0, 0))
    return pl.pallas_call(
        _ln_in_kernel,
        grid=(n // bm,),
        in_specs=[row, vec, vec],
        out_specs=[row, row],
        out_shape=[jax.ShapeDtypeStruct((n, d), F32), jax.ShapeDtypeStruct((n, d), BF16)],
        compiler_params=_params("parallel"),
        name="ln_in",
    )(x, g.reshape(1, d), b.reshape(1, d))


def _mm_kernel(a_ref, w_ref, o_ref):
    o_ref[...] = jnp.dot(a_ref[...], w_ref[...], preferred_element_type=F32).astype(o_ref.dtype)


def matmul(a, w, bm=1024, bn=1024, name="matmul"):
    m, k = a.shape
    _, n = w.shape
    return pl.pallas_call(
        _mm_kernel,
        grid=(m // bm, n // bn),
        in_specs=[pl.BlockSpec((bm, k), lambda i, j: (i, 0)),
                  pl.BlockSpec((k, bn), lambda i, j: (0, j))],
        out_specs=pl.BlockSpec((bm, bn), lambda i, j: (i, j)),
        out_shape=jax.ShapeDtypeStruct((m, n), BF16),
        compiler_params=_params("parallel", "arbitrary"),
        name=name,
    )(a, w)


def _mla_proj_kernel(cq_ref, ckv_ref, kr_ref, cos_ref, sin_ref, qg_ref, kvg_ref, wq_ref, wkv_ref,
                     q_ref, k_ref, v_ref, cqn_ref, ckvn_ref, krr_ref, *, scale):
    h = pl.program_id(1)

    @pl.when(h == 0)
    def _():
        cq = cq_ref[...].astype(F32)
        cqn_ref[...] = (cq * lax.rsqrt(jnp.mean(cq * cq, axis=-1, keepdims=True) + RMS_EPS)
                        * qg_ref[...]).astype(BF16)
        ckv = ckv_ref[...].astype(F32)
        ckvn_ref[...] = (ckv * lax.rsqrt(jnp.mean(ckv * ckv, axis=-1, keepdims=True) + RMS_EPS)
                         * kvg_ref[...]).astype(BF16)
        kr = kr_ref[...].astype(F32)
        krr_ref[...] = (kr[:, :MLA_ROPE_DIM] * cos_ref[...]
                        + kr[:, MLA_ROPE_DIM:] * sin_ref[...]).astype(BF16)

    qh = jnp.dot(cqn_ref[...], wq_ref[0], preferred_element_type=F32)
    q_ref[0, 0, :, :MLA_NOPE_DIM] = (qh[:, :MLA_NOPE_DIM] * scale).astype(BF16)
    q_rope = (qh[:, MLA_NOPE_DIM:MLA_QK_DIM] * cos_ref[...]
              + qh[:, MLA_QK_DIM:] * sin_ref[...])
    q_ref[0, 0, :, MLA_NOPE_DIM:] = (q_rope * scale).astype(BF16)
    kvh = jnp.dot(ckvn_ref[...], wkv_ref[0], preferred_element_type=F32)
    k_ref[0, 0, :, :MLA_NOPE_DIM] = kvh[:, :MLA_NOPE_DIM].astype(BF16)
    k_ref[0, 0, :, MLA_NOPE_DIM:] = krr_ref[...]
    v_ref[0, 0] = kvh[:, MLA_NOPE_DIM:].astype(BF16)


def mla_proj(h, cos2, sin2, qg, kvg, wq, wkv, batch, seq, bm=512):
    n = h.shape[0]
    spb = seq // bm
    H = MLA_HEADS
    scale = MLA_QK_DIM ** -0.5
    out_map = lambda i, hh: (i // spb, hh, i % spb, 0)
    return pl.pallas_call(
        functools.partial(_mla_proj_kernel, scale=scale),
        grid=(n // bm, H),
        in_specs=[
            pl.BlockSpec((bm, MLA_Q_LORA), lambda i, hh: (i, COL_CQ // MLA_Q_LORA)),
            pl.BlockSpec((bm, MLA_KV_LORA), lambda i, hh: (i, COL_CKV // MLA_KV_LORA)),
            pl.BlockSpec((bm, 2 * MLA_ROPE_DIM), lambda i, hh: (i, COL_KR // (2 * MLA_ROPE_DIM))),
            pl.BlockSpec((bm, MLA_ROPE_DIM), lambda i, hh: (i, 0)),
            pl.BlockSpec((bm, MLA_ROPE_DIM), lambda i, hh: (i, 0)),
            pl.BlockSpec((1, MLA_Q_LORA), lambda i, hh: (0, 0)),
            pl.BlockSpec((1, MLA_KV_LORA), lambda i, hh: (0, 0)),
            pl.BlockSpec((1, MLA_Q_LORA, 256), lambda i, hh: (hh, 0, 0)),
            pl.BlockSpec((1, MLA_KV_LORA, 256), lambda i, hh: (hh, 0, 0)),
        ],
        out_specs=[
            pl.BlockSpec((1, 1, bm, MLA_QK_DIM), out_map),
            pl.BlockSpec((1, 1, bm, MLA_QK_DIM), out_map),
            pl.BlockSpec((1, 1, bm, MLA_V_DIM), out_map),
        ],
        out_shape=[
            jax.ShapeDtypeStruct((batch, H, seq, MLA_QK_DIM), BF16),
            jax.ShapeDtypeStruct((batch, H, seq, MLA_QK_DIM), BF16),
            jax.ShapeDtypeStruct((batch, H, seq, MLA_V_DIM), BF16),
        ],
        scratch_shapes=[
            pltpu.VMEM((bm, MLA_Q_LORA), BF16),
            pltpu.VMEM((bm, MLA_KV_LORA), BF16),
            pltpu.VMEM((bm, MLA_ROPE_DIM), BF16),
        ],
        compiler_params=_params("parallel", "arbitrary"),
        name="mla_proj",
    )(h, h, h, cos2, sin2, qg, kvg, wq, wkv)


def _attn_kernel(q_ref, k_ref, v_ref, o_ref, *, tq):
    qi = pl.program_id(2)
    q = q_ref[0, 0]

    def tile(j, carry, masked):
        m, l, acc = carry
        start = pl.multiple_of(j * tq, tq)
        k = k_ref[0, 0, pl.ds(start, tq), :]
        v = v_ref[0, 0, pl.ds(start, tq), :]
        s = lax.dot_general(q, k, (((1,), (1,)), ((), ())), preferred_element_type=F32)
        if masked:
            qc = lax.broadcasted_iota(jnp.int32, (tq, tq), 0) // CHUNK
            kc = lax.broadcasted_iota(jnp.int32, (tq, tq), 1) // CHUNK
            s = jnp.where(qc >= kc, s, MASK_VALUE)
        m_new = jnp.maximum(m, jnp.max(s, axis=-1, keepdims=True))
        p = jnp.exp(s - m_new)
        alpha = jnp.exp(m - m_new)
        l = alpha * l + jnp.sum(p, axis=-1, keepdims=True)
        acc = alpha * acc + jnp.dot(p.astype(BF16), v, preferred_element_type=F32)
        return m_new, l, acc

    init = (jnp.full((tq, 1), MASK_VALUE, F32), jnp.zeros((tq, 1), F32),
            jnp.zeros((tq, MLA_V_DIM), F32))
    carry = lax.fori_loop(0, qi, lambda j, c: tile(j, c, False), init)
    m, l, acc = tile(qi, carry, True)
    o_ref[...] = (acc / l).astype(BF16)


def mla_attention(q, k, v, tq=256):
    batch, H, seq, _ = q.shape
    nq = seq // tq
    return pl.pallas_call(
        functools.partial(_attn_kernel, tq=tq),
        grid=(batch, H, nq),
        in_specs=[
            pl.BlockSpec((1, 1, tq, MLA_QK_DIM), lambda b, h, i: (b, h, i, 0)),
            pl.BlockSpec((1, 1, seq, MLA_QK_DIM), lambda b, h, i: (b, h, 0, 0)),
            pl.BlockSpec((1, 1, seq, MLA_V_DIM), lambda b, h, i: (b, h, 0, 0)),
        ],
        out_specs=pl.BlockSpec((tq, MLA_V_DIM), lambda b, h, i: (b * nq + i, h)),
        out_shape=jax.ShapeDtypeStruct((batch * seq, H * MLA_V_DIM), BF16),
        compiler_params=_params("parallel", "parallel", "arbitrary"),
        name="mla_attention",
    )(q, k, v)


def _ret_kernel(lg_ref, rq_ref, rk_ref, rv_ref, rg_ref, cos_ref, sin_ref, gng_ref, gnb_ref,
                o_ref, state_ref, *, T):
    h = pl.program_id(1)
    c = pl.program_id(2)
    lg = lg_ref[h]
    half = RET_QK_DIM // 2

    @pl.when(c == 0)
    def _():
        state_ref[...] = jnp.zeros_like(state_ref)

    cos = cos_ref[...]
    sin = sin_ref[...]

    def rope(t):
        t1, t2 = t[:, :half], t[:, half:]
        return jnp.concatenate([t1 * cos - t2 * sin, t2 * cos + t1 * sin], axis=-1)

    q = rope(rq_ref[...].astype(F32)) * (RET_QK_DIM ** -0.5)
    k = rope(rk_ref[...].astype(F32))
    v = rv_ref[...]

    row = lax.broadcasted_iota(jnp.int32, (T, T), 0)
    col = lax.broadcasted_iota(jnp.int32, (T, T), 1)
    dist = jnp.abs(row - col).astype(F32)
    decay = jnp.where(row // CHUNK >= col // CHUNK, jnp.exp(lg * dist), 0.0)
    pos = lax.broadcasted_iota(jnp.int32, (T, 1), 0).astype(F32)
    q_decay = jnp.exp(lg * (pos + 1.0))
    k_decay = jnp.exp(lg * (T - 1.0 - pos))

    qb = q.astype(BF16)
    s = lax.dot_general(qb, k.astype(BF16), (((1,), (1,)), ((), ())), preferred_element_type=F32)
    o = jnp.dot((s * decay).astype(BF16), v, preferred_element_type=F32)
    state = state_ref[...]
    o = o + jnp.dot((q * q_decay).astype(BF16), state.astype(BF16), preferred_element_type=F32)
    kv = lax.dot_general((k * k_decay).astype(BF16), v, (((0,), (0,)), ((), ())),
                         preferred_element_type=F32)
    state_ref[...] = state * jnp.exp(lg * T) + kv

    mu = jnp.mean(o, axis=-1, keepdims=True)
    d = o - mu
    var = jnp.mean(d * d, axis=-1, keepdims=True)
    o = d * lax.rsqrt(var + GN_EPS) * gng_ref[...] + gnb_ref[...]
    g = rg_ref[...].astype(F32)
    o_ref[...] = (g * jax.nn.sigmoid(g) * o).astype(BF16)


def retention(h, log_gamma, cos_r, sin_r, gn_g, gn_b, batch, seq, T=256):
    n = h.shape[0]
    H = RET_HEADS
    nc = seq // T
    W = RET_QK_DIM

    def col(base):
        return pl.BlockSpec((T, W), lambda b, hh, c: (b * nc + c, base // W + hh))

    tab = pl.BlockSpec((T, W // 2), lambda b, hh, c: (b * nc + c, 0))
    vec = pl.BlockSpec((1, W), lambda b, hh, c: (0, hh))
    return pl.pallas_call(
        functools.partial(_ret_kernel, T=T),
        grid=(batch, H, nc),
        in_specs=[pl.BlockSpec(memory_space=pltpu.SMEM),
                  col(COL_RQ), col(COL_RK), col(COL_RV), col(COL_RG), tab, tab, vec, vec],
        out_specs=pl.BlockSpec((T, W), lambda b, hh, c: (b * nc + c, hh)),
        out_shape=jax.ShapeDtypeStruct((n, H * RET_V_DIM), BF16),
        scratch_shapes=[pltpu.VMEM((RET_QK_DIM, RET_V_DIM), F32)],
        compiler_params=_params("parallel", "parallel", "arbitrary"),
        name="retention",
    )(log_gamma, h, h, h, h, cos_r, sin_r, gn_g.reshape(1, -1), gn_b.reshape(1, -1))


def _out_ln_kernel(a_ref, r_ref, wa_ref, wr_ref, x_ref, g_ref, b_ref, of_ref, ob_ref, *, alpha):
    mix = jnp.dot(a_ref[...], wa_ref[...], preferred_element_type=F32)
    mix = mix + jnp.dot(r_ref[...], wr_ref[...], preferred_element_type=F32)
    y = _layer_norm(alpha * x_ref[...] + mix, g_ref[...], b_ref[...])
    of_ref[...] = y
    ob_ref[...] = y.astype(BF16)


def out_proj_ln(a, r, w_a, w_r, x, g, b, alpha, bm=512):
    n, d = x.shape
    ka, kr = a.shape[1], r.shape[1]
    row = pl.BlockSpec((bm, d), lambda i: (i, 0))
    vec = pl.BlockSpec((1, d), lambda i: (0, 0))
    return pl.pallas_call(
        functools.partial(_out_ln_kernel, alpha=alpha),
        grid=(n // bm,),
        in_specs=[pl.BlockSpec((bm, ka), lambda i: (i, 0)),
                  pl.BlockSpec((bm, kr), lambda i: (i, 0)),
                  pl.BlockSpec((ka, d), lambda i: (0, 0)),
                  pl.BlockSpec((kr, d), lambda i: (0, 0)),
                  row, vec, vec],
        out_specs=[row, row],
        out_shape=[jax.ShapeDtypeStruct((n, d), F32), jax.ShapeDtypeStruct((n, d), BF16)],
        compiler_params=_params("parallel"),
        name="out_proj_ln",
    )(a, r, w_a, w_r, x, g.reshape(1, d), b.reshape(1, d))


def _ffn_up_kernel(x_ref, wg_ref, wu_ref, o_ref):
    x = x_ref[...]
    g = jnp.dot(x, wg_ref[...], preferred_element_type=F32)
    u = jnp.dot(x, wu_ref[...], preferred_element_type=F32)
    o_ref[...] = (g * jax.nn.sigmoid(g) * u).astype(BF16)


def ffn_up(x, wg, wu, bm=1024, bn=512):
    m, k = x.shape
    n = wg.shape[1]
    wspec = pl.BlockSpec((k, bn), lambda i, j: (0, j))
    return pl.pallas_call(
        _ffn_up_kernel,
        grid=(m // bm, n // bn),
        in_specs=[pl.BlockSpec((bm, k), lambda i, j: (i, 0)), wspec, wspec],
        out_specs=pl.BlockSpec((bm, bn), lambda i, j: (i, j)),
        out_shape=jax.ShapeDtypeStruct((m, n), BF16),
        compiler_params=_params("parallel", "arbitrary"),
        name="ffn_up",
    )(x, wg, wu)


def _ffn_down_ln_kernel(h_ref, w_ref, x_ref, g_ref, b_ref, of_ref, ob_ref, acc_ref, *, alpha):
    kk = pl.program_id(1)

    @pl.when(kk == 0)
    def _():
        acc_ref[...] = jnp.zeros_like(acc_ref)

    acc_ref[...] += jnp.dot(h_ref[...], w_ref[...], preferred_element_type=F32)

    @pl.when(kk == pl.num_programs(1) - 1)
    def _():
        y = _layer_norm(alpha * x_ref[...] + acc_ref[...], g_ref[...], b_ref[...])
        of_ref[...] = y
        ob_ref[...] = y.astype(BF16)


def ffn_down_ln(hm, w, x, g, b, alpha, bm=512, bk=512):
    n, d = x.shape
    kdim = hm.shape[1]
    row = pl.BlockSpec((bm, d), lambda i, k: (i, 0))
    vec = pl.BlockSpec((1, d), lambda i, k: (0, 0))
    return pl.pallas_call(
        functools.partial(_ffn_down_ln_kernel, alpha=alpha),
        grid=(n // bm, kdim // bk),
        in_specs=[pl.BlockSpec((bm, bk), lambda i, k: (i, k)),
                  pl.BlockSpec((bk, d), lambda i, k: (k, 0)),
                  row, vec, vec],
        out_specs=[row, row],
        out_shape=[jax.ShapeDtypeStruct((n, d), F32), jax.ShapeDtypeStruct((n, d), BF16)],
        scratch_shapes=[pltpu.VMEM((bm, d), F32)],
        compiler_params=_params("parallel", "arbitrary"),
        name="ffn_down_ln",
    )(hm, w, x, g.reshape(1, d), b.reshape(1, d))


def _rotate_half_cols(w):
    half = w.shape[-1] // 2
    return jnp.concatenate([-w[..., half:], w[..., :half]], axis=-1)


def _prep_w_in(w):
    kr = w[:, 768:832]
    pad = jnp.zeros((w.shape[0], COL_RQ - COL_KR - 2 * MLA_ROPE_DIM), w.dtype)
    return jnp.concatenate([w[:, :768], kr, _rotate_half_cols(kr), pad, w[:, 832:]],
                           axis=1).astype(BF16)


def _prep_w_uq(w):
    w = w.reshape(MLA_Q_LORA, MLA_HEADS, MLA_QK_DIM)
    rope = w[..., MLA_NOPE_DIM:]
    w = jnp.concatenate([w, _rotate_half_cols(rope)], axis=-1)
    return w.transpose(1, 0, 2).astype(BF16)


def _prep_w_ukv(w):
    w = w.reshape(MLA_KV_LORA, MLA_HEADS, MLA_NOPE_DIM + MLA_V_DIM)
    return w.transpose(1, 0, 2).astype(BF16)


def _rope_tables(positions, dim):
    inv_freq = ROPE_THETA ** (-jnp.arange(0, dim, 2, dtype=F32) / dim)
    ang = positions.astype(F32)[..., None] * inv_freq
    return jnp.cos(ang), jnp.sin(ang)


def kernel(x, positions, ln_in_g, ln_in_b, w_in, q_norm_g, kv_norm_g, w_uq, w_ukv, ret_gn_g, ret_gn_b, w_out, ln1_g, ln1_b, w_gate, w_up, w_down, ln2_g, ln2_b):
    batch, seq, d = x.shape
    depth = w_in.shape[0]
    n = batch * seq
    alpha = (2 * depth) ** 0.25

    cos_m, sin_m = _rope_tables(positions, MLA_ROPE_DIM)
    cos_r, sin_r = _rope_tables(positions, RET_QK_DIM)
    cos2 = jnp.concatenate([cos_m, cos_m], axis=-1).reshape(n, MLA_ROPE_DIM)
    sin2 = jnp.concatenate([sin_m, sin_m], axis=-1).reshape(n, MLA_ROPE_DIM)
    cos_r = cos_r.reshape(n, RET_QK_DIM // 2)
    sin_r = sin_r.reshape(n, RET_QK_DIM // 2)
    log_gamma = jnp.log1p(-jnp.exp2(-5.0 - jnp.arange(RET_HEADS, dtype=F32)))

    xf, xb = ln_in(x.reshape(n, d), ln_in_g, ln_in_b)
    a_width = MLA_HEADS * MLA_V_DIM
    for l in range(depth):
        h = matmul(xb, _prep_w_in(w_in[l]), name="in_proj")
        q, k, v = mla_proj(h, cos2, sin2, q_norm_g[l].reshape(1, -1), kv_norm_g[l].reshape(1, -1),
                           _prep_w_uq(w_uq[l]), _prep_w_ukv(w_ukv[l]), batch, seq)
        a = mla_attention(q, k, v)
        r = retention(h, log_gamma, cos_r, sin_r, ret_gn_g[l], ret_gn_b[l], batch, seq)
        wo = w_out[l].astype(BF16)
        xf, xb = out_proj_ln(a, r, wo[:a_width], wo[a_width:], xf, ln1_g[l], ln1_b[l], alpha)
        hm = ffn_up(xb, w_gate[l].astype(BF16), w_up[l].astype(BF16))
        xf, xb = ffn_down_ln(hm, w_down[l].astype(BF16), xf, ln2_g[l], ln2_b[l], alpha)
    return xf.reshape(batch, seq, d)
```

```python
import functools

import jax
import jax.numpy as jnp
from jax import lax
from jax.experimental import pallas as pl
from jax.experimental.pallas import tpu as pltpu

F32 = jnp.float32
BF16 = jnp.bfloat16

D_MODEL = 2048
CHUNK = 64
MLA_HEADS = 8
MLA_Q_LORA = 512
MLA_KV_LORA = 256
MLA_NOPE_DIM = 128
MLA_ROPE_DIM = 64
MLA_V_DIM = 128
MLA_QK_DIM = MLA_NOPE_DIM + MLA_ROPE_DIM
RET_HEADS = 4
RET_QK_DIM = 256
RET_V_DIM = 256
ROPE_THETA = 10000.0
LN_EPS = 1e-5
RMS_EPS = 1e-6
GN_EPS = 1e-5

COL_CQ = 0
COL_CKV = 512
COL_KR = 768
COL_RQ = 1024
COL_RK = 2048
COL_RV = 3072
COL_RG = 4096
D_IN_PAD = 5120

VMEM_LIMIT = 56 * 1024 * 1024

MASK_VALUE = -1e30
LOG2E = 1.4426950408889634
FFN_SUB = 256


def _params(*sem):
    return pltpu.CompilerParams(dimension_semantics=sem, vmem_limit_bytes=VMEM_LIMIT)


def _layer_norm(y, g, b):
    mu = jnp.mean(y, axis=-1, keepdims=True)
    d = y - mu
    var = jnp.mean(d * d, axis=-1, keepdims=True)
    return d * lax.rsqrt(var + LN_EPS) * g + b


def _ln_in_kernel(x_ref, g_ref, b_ref, of_ref, ob_ref):
    y = _layer_norm(x_ref[...], g_ref[...], b_ref[...])
    of_ref[...] = y
    ob_ref[...] = y.astype(BF16)


def ln_in(x, g, b, bm=512):
    n, d = x.shape
    row = pl.BlockSpec((bm, d), lambda i: (i, 0))
    vec = pl.BlockSpec((1, d), lambda i: (0, 0))
    return pl.pallas_call(
        _ln_in_kernel,
        grid=(n // bm,),
        in_specs=[row, vec, vec],
        out_specs=[row, row],
        out_shape=[jax.ShapeDtypeStruct((n, d), F32), jax.ShapeDtypeStruct((n, d), BF16)],
        compiler_params=_params("parallel"),
        name="ln_in",
    )(x, g.reshape(1, d), b.reshape(1, d))


def _mm_kernel(a_ref, w_ref, o_ref):
    o_ref[...] = jnp.dot(a_ref[...], w_ref[...], preferred_element_type=F32).astype(o_ref.dtype)


def matmul(a, w, bm=1024, bn=1024, name="matmul"):
    m, k = a.shape
    _, n = w.shape
    return pl.pallas_call(
        _mm_kernel,
        grid=(m // bm, n // bn),
        in_specs=[pl.BlockSpec((bm, k), lambda i, j: (i, 0)),
                  pl.BlockSpec((k, bn), lambda i, j: (0, j))],
        out_specs=pl.BlockSpec((bm, bn), lambda i, j: (i, j)),
        out_shape=jax.ShapeDtypeStruct((m, n), BF16),
        compiler_params=_params("parallel", "arbitrary"),
        name=name,
    )(a, w)


def _rms_norm(t, g):
    return t * lax.rsqrt(jnp.mean(t * t, axis=-1, keepdims=True) + RMS_EPS) * g


def _mla_proj_kernel(cq_ref, ckv_ref, kr_ref, cos_ref, sin_ref, qg_ref, kvg_ref, wq_ref, wkv_ref,
                     q_ref, kt_ref, v_ref, *, q_scale):
    cos = cos_ref[...]
    sin = sin_ref[...]
    cqn = _rms_norm(cq_ref[...].astype(F32), qg_ref[...]).astype(BF16)
    ckvn = _rms_norm(ckv_ref[...].astype(F32), kvg_ref[...]).astype(BF16)
    kr = kr_ref[...].astype(F32)
    kr_t = (kr[:, :MLA_ROPE_DIM] * cos + kr[:, MLA_ROPE_DIM:] * sin).T.astype(BF16)
    ones = jnp.ones((cqn.shape[0], MLA_V_DIM), BF16)
    for h in range(MLA_HEADS):
        qh = jnp.dot(cqn, wq_ref[h], preferred_element_type=F32)
        q_ref[0, h, :, :MLA_NOPE_DIM] = (qh[:, :MLA_NOPE_DIM] * q_scale).astype(BF16)
        q_rope = qh[:, MLA_NOPE_DIM:MLA_QK_DIM] * cos + qh[:, MLA_QK_DIM:] * sin
        q_ref[0, h, :, MLA_NOPE_DIM:] = (q_rope * q_scale).astype(BF16)
        kvh = jnp.dot(ckvn, wkv_ref[h], preferred_element_type=F32)
        kt_ref[0, h, :MLA_NOPE_DIM, :] = kvh[:, :MLA_NOPE_DIM].T.astype(BF16)
        kt_ref[0, h, MLA_NOPE_DIM:, :] = kr_t
        v_ref[0, h, :, :MLA_V_DIM] = kvh[:, MLA_NOPE_DIM:].astype(BF16)
        v_ref[0, h, :, MLA_V_DIM:] = ones


def mla_proj(h, cos2, sin2, qg, kvg, wq, wkv, batch, seq, bm=512):
    n = h.shape[0]
    spb = seq // bm
    H = MLA_HEADS
    q_scale = (MLA_QK_DIM ** -0.5) * LOG2E
    full = lambda shape: pl.BlockSpec(shape, lambda i: (0,) * len(shape))
    return pl.pallas_call(
        functools.partial(_mla_proj_kernel, q_scale=q_scale),
        grid=(n // bm,),
        in_specs=[
            pl.BlockSpec((bm, MLA_Q_LORA), lambda i: (i, COL_CQ // MLA_Q_LORA)),
            pl.BlockSpec((bm, MLA_KV_LORA), lambda i: (i, COL_CKV // MLA_KV_LORA)),
            pl.BlockSpec((bm, 2 * MLA_ROPE_DIM), lambda i: (i, COL_KR // (2 * MLA_ROPE_DIM))),
            pl.BlockSpec((bm, MLA_ROPE_DIM), lambda i: (i, 0)),
            pl.BlockSpec((bm, MLA_ROPE_DIM), lambda i: (i, 0)),
            full((1, MLA_Q_LORA)),
            full((1, MLA_KV_LORA)),
            full((H, MLA_Q_LORA, 256)),
            full((H, MLA_KV_LORA, 256)),
        ],
        out_specs=[
            pl.BlockSpec((1, H, bm, MLA_QK_DIM), lambda i: (i // spb, 0, i % spb, 0)),
            pl.BlockSpec((1, H, MLA_QK_DIM, bm), lambda i: (i // spb, 0, 0, i % spb)),
            pl.BlockSpec((1, H, bm, 2 * MLA_V_DIM), lambda i: (i // spb, 0, i % spb, 0)),
        ],
        out_shape=[
            jax.ShapeDtypeStruct((batch, H, seq, MLA_QK_DIM), BF16),
            jax.ShapeDtypeStruct((batch, H, MLA_QK_DIM, seq), BF16),
            jax.ShapeDtypeStruct((batch, H, seq, 2 * MLA_V_DIM), BF16),
        ],
        compiler_params=_params("parallel"),
        name="mla_proj",
    )(h, h, h, cos2, sin2, qg, kvg, wq, wkv)


def _attn_kernel(q_ref, kt_ref, v_ref, o_ref, *, t):
    qi = pl.program_id(2)
    q = q_ref[0, 0]

    def scores(j):
        start = pl.multiple_of(j * t, t)
        return jnp.dot(q, kt_ref[0, 0, :, pl.ds(start, t)], preferred_element_type=F32)

    def update(j, s, m, acc):
        start = pl.multiple_of(j * t, t)
        v1 = v_ref[0, 0, pl.ds(start, t), :]
        m_new = jnp.maximum(m, jnp.max(s, axis=-1, keepdims=True))
        p = jnp.exp2(s - m_new)
        acc = jnp.exp2(m - m_new) * acc + jnp.dot(p.astype(BF16), v1, preferred_element_type=F32)
        return m_new, acc

    def body(j, carry):
        s, m, acc = carry
        s_next = scores(j + 1)
        m, acc = update(j, s, m, acc)
        return s_next, m, acc

    init = (scores(0), jnp.full((t, 1), MASK_VALUE, F32), jnp.zeros((t, 2 * MLA_V_DIM), F32))
    s, m, acc = lax.fori_loop(0, qi, body, init)
    qc = lax.broadcasted_iota(jnp.int32, (t, t), 0) // CHUNK
    kc = lax.broadcasted_iota(jnp.int32, (t, t), 1) // CHUNK
    m, acc = update(qi, jnp.where(qc >= kc, s, MASK_VALUE), m, acc)
    o_ref[...] = (acc[:, :MLA_V_DIM] / acc[:, MLA_V_DIM:]).astype(BF16)


def mla_attention(q, kt, v1, t=512):
    batch, H, seq, _ = q.shape
    nq = seq // t
    return pl.pallas_call(
        functools.partial(_attn_kernel, t=t),
        grid=(batch, H, nq),
        in_specs=[
            pl.BlockSpec((1, 1, t, MLA_QK_DIM), lambda b, h, i: (b, h, i, 0)),
            pl.BlockSpec((1, 1, MLA_QK_DIM, seq), lambda b, h, i: (b, h, 0, 0)),
            pl.BlockSpec((1, 1, seq, 2 * MLA_V_DIM), lambda b, h, i: (b, h, 0, 0)),
        ],
        out_specs=pl.BlockSpec((t, MLA_V_DIM), lambda b, h, i: (b * nq + i, h)),
        out_shape=jax.ShapeDtypeStruct((batch * seq, H * MLA_V_DIM), BF16),
        compiler_params=_params("parallel", "parallel", "arbitrary"),
        name="mla_attention",
    )(q, kt, v1)


def _ret_kernel(lg_ref, rq_ref, rk_ref, rv_ref, rg_ref, cos_ref, sin_ref, gng_ref, gnb_ref,
                o_ref, state_ref, *, T):
    h = pl.program_id(1)
    c = pl.program_id(2)
    lg = lg_ref[h]
    half = RET_QK_DIM // 2

    @pl.when(c == 0)
    def _():
        state_ref[...] = jnp.zeros_like(state_ref)

    cos = cos_ref[...]
    sin = sin_ref[...]

    def rope(t):
        t1, t2 = t[:, :half], t[:, half:]
        return jnp.concatenate([t1 * cos - t2 * sin, t2 * cos + t1 * sin], axis=-1)

    q = rope(rq_ref[...].astype(F32)) * (RET_QK_DIM ** -0.5)
    k = rope(rk_ref[...].astype(F32))
    v = rv_ref[...]

    row = lax.broadcasted_iota(jnp.int32, (T, T), 0)
    col = lax.broadcasted_iota(jnp.int32, (T, T), 1)
    dist = jnp.abs(row - col).astype(F32)
    decay = jnp.where(row // CHUNK >= col // CHUNK, jnp.exp(lg * dist), 0.0)
    pos = lax.broadcasted_iota(jnp.int32, (T, 1), 0).astype(F32)
    q_decay = jnp.exp(lg * (pos + 1.0))
    k_decay = jnp.exp(lg * (T - 1.0 - pos))

    qb = q.astype(BF16)
    s = lax.dot_general(qb, k.astype(BF16), (((1,), (1,)), ((), ())), preferred_element_type=F32)
    o = jnp.dot((s * decay).astype(BF16), v, preferred_element_type=F32)
    state = state_ref[...]
    o = o + jnp.dot((q * q_decay).astype(BF16), state.astype(BF16), preferred_element_type=F32)
    kv = lax.dot_general((k * k_decay).astype(BF16), v, (((0,), (0,)), ((), ())),
                         preferred_element_type=F32)
    state_ref[...] = state * jnp.exp(lg * T) + kv

    mu = jnp.mean(o, axis=-1, keepdims=True)
    d = o - mu
    var = jnp.mean(d * d, axis=-1, keepdims=True)
    o = d * lax.rsqrt(var + GN_EPS) * gng_ref[...] + gnb_ref[...]
    g = rg_ref[...].astype(F32)
    o_ref[...] = (g * jax.nn.sigmoid(g) * o).astype(BF16)


def retention(h, log_gamma, cos_r, sin_r, gn_g, gn_b, batch, seq, T=256):
    n = h.shape[0]
    H = RET_HEADS
    nc = seq // T
    W = RET_QK_DIM

    def col(base):
        return pl.BlockSpec((T, W), lambda b, hh, c: (b * nc + c, base // W + hh))

    tab = pl.BlockSpec((T, W // 2), lambda b, hh, c: (b * nc + c, 0))
    vec = pl.BlockSpec((1, W), lambda b, hh, c: (0, hh))
    return pl.pallas_call(
        functools.partial(_ret_kernel, T=T),
        grid=(batch, H, nc),
        in_specs=[pl.BlockSpec(memory_space=pltpu.SMEM),
                  col(COL_RQ), col(COL_RK), col(COL_RV), col(COL_RG), tab, tab, vec, vec],
        out_specs=pl.BlockSpec((T, W), lambda b, hh, c: (b * nc + c, hh)),
        out_shape=jax.ShapeDtypeStruct((n, H * RET_V_DIM), BF16),
        scratch_shapes=[pltpu.VMEM((RET_QK_DIM, RET_V_DIM), F32)],
        compiler_params=_params("parallel", "parallel", "arbitrary"),
        name="retention",
    )(log_gamma, h, h, h, h, cos_r, sin_r, gn_g.reshape(1, -1), gn_b.reshape(1, -1))


def _out_ln_kernel(a_ref, r_ref, wa_ref, wr_ref, x_ref, g_ref, b_ref, of_ref, ob_ref, *, alpha):
    mix = jnp.dot(a_ref[...], wa_ref[...], preferred_element_type=F32)
    mix = mix + jnp.dot(r_ref[...], wr_ref[...], preferred_element_type=F32)
    y = _layer_norm(alpha * x_ref[...] + mix, g_ref[...], b_ref[...])
    of_ref[...] = y
    ob_ref[...] = y.astype(BF16)


def out_proj_ln(a, r, w_a, w_r, x, g, b, alpha, bm=512):
    n, d = x.shape
    ka, kr = a.shape[1], r.shape[1]
    row = pl.BlockSpec((bm, d), lambda i: (i, 0))
    vec = pl.BlockSpec((1, d), lambda i: (0, 0))
    return pl.pallas_call(
        functools.partial(_out_ln_kernel, alpha=alpha),
        grid=(n // bm,),
        in_specs=[pl.BlockSpec((bm, ka), lambda i: (i, 0)),
                  pl.BlockSpec((bm, kr), lambda i: (i, 0)),
                  pl.BlockSpec((ka, d), lambda i: (0, 0)),
                  pl.BlockSpec((kr, d), lambda i: (0, 0)),
                  row, vec, vec],
        out_specs=[row, row],
        out_shape=[jax.ShapeDtypeStruct((n, d), F32), jax.ShapeDtypeStruct((n, d), BF16)],
        compiler_params=_params("parallel"),
        name="out_proj_ln",
    )(a, r, w_a, w_r, x, g.reshape(1, d), b.reshape(1, d))


def _ffn_ln_kernel(xb_ref, wg_ref, wu_ref, wd_ref, x_ref, g_ref, b_ref, of_ref, ob_ref, acc_ref,
                   *, alpha):
    j = pl.program_id(1)

    @pl.when(j == 0)
    def _():
        acc_ref[...] = jnp.zeros_like(acc_ref)

    xb = xb_ref[...]
    part = None
    for s in range(wg_ref.shape[1] // FFN_SUB):
        cols = slice(s * FFN_SUB, (s + 1) * FFN_SUB)
        g = jnp.dot(xb, wg_ref[:, cols], preferred_element_type=F32)
        u = jnp.dot(xb, wu_ref[:, cols], preferred_element_type=F32)
        hm = (g * jax.nn.sigmoid(g) * u).astype(BF16)
        d = jnp.dot(hm, wd_ref[cols, :], preferred_element_type=F32)
        part = d if part is None else part + d
    acc_ref[...] += part

    @pl.when(j == pl.num_programs(1) - 1)
    def _():
        y = _layer_norm(alpha * x_ref[...] + acc_ref[...], g_ref[...], b_ref[...])
        of_ref[...] = y
        ob_ref[...] = y.astype(BF16)


def ffn_ln(xb, wg, wu, wd, x, g, b, alpha, bm=512, bn=512):
    n, d = x.shape
    dff = wg.shape[1]
    row = pl.BlockSpec((bm, d), lambda i, j: (i, 0))
    vec = pl.BlockSpec((1, d), lambda i, j: (0, 0))
    wcol = pl.BlockSpec((d, bn), lambda i, j: (0, j))
    return pl.pallas_call(
        functools.partial(_ffn_ln_kernel, alpha=alpha),
        grid=(n // bm, dff // bn),
        in_specs=[row, wcol, wcol, pl.BlockSpec((bn, d), lambda i, j: (j, 0)), row, vec, vec],
        out_specs=[row, row],
        out_shape=[jax.ShapeDtypeStruct((n, d), F32), jax.ShapeDtypeStruct((n, d), BF16)],
        scratch_shapes=[pltpu.VMEM((bm, d), F32)],
        compiler_params=_params("parallel", "arbitrary"),
        name="ffn_ln",
    )(xb, wg, wu, wd, x, g.reshape(1, d), b.reshape(1, d))


def _rotate_half_cols(w):
    half = w.shape[-1] // 2
    return jnp.concatenate([-w[..., half:], w[..., :half]], axis=-1)


def _prep_w_in(w):
    kr = w[:, 768:832]
    pad = jnp.zeros((w.shape[0], COL_RQ - COL_KR - 2 * MLA_ROPE_DIM), w.dtype)
    return jnp.concatenate([w[:, :768], kr, _rotate_half_cols(kr), pad, w[:, 832:]],
                           axis=1).astype(BF16)


def _prep_w_uq(w):
    w = w.reshape(MLA_Q_LORA, MLA_HEADS, MLA_QK_DIM)
    rope = w[..., MLA_NOPE_DIM:]
    w = jnp.concatenate([w, _rotate_half_cols(rope)], axis=-1)
    return w.transpose(1, 0, 2).astype(BF16)


def _prep_w_ukv(w):
    w = w.reshape(MLA_KV_LORA, MLA_HEADS, MLA_NOPE_DIM + MLA_V_DIM)
    return w.transpose(1, 0, 2).astype(BF16)


def _rope_tables(positions, dim):
    inv_freq = ROPE_THETA ** (-jnp.arange(0, dim, 2, dtype=F32) / dim)
    ang = positions.astype(F32)[..., None] * inv_freq
    return jnp.cos(ang), jnp.sin(ang)


def kernel(x, positions, ln_in_g, ln_in_b, w_in, q_norm_g, kv_norm_g, w_uq, w_ukv, ret_gn_g, ret_gn_b, w_out, ln1_g, ln1_b, w_gate, w_up, w_down, ln2_g, ln2_b):
    batch, seq, d = x.shape
    depth = w_in.shape[0]
    n = batch * seq
    alpha = (2 * depth) ** 0.25

    cos_m, sin_m = _rope_tables(positions, MLA_ROPE_DIM)
    cos_r, sin_r = _rope_tables(positions, RET_QK_DIM)
    cos2 = jnp.concatenate([cos_m, cos_m], axis=-1).reshape(n, MLA_ROPE_DIM)
    sin2 = jnp.concatenate([sin_m, sin_m], axis=-1).reshape(n, MLA_ROPE_DIM)
    cos_r = cos_r.reshape(n, RET_QK_DIM // 2)
    sin_r = sin_r.reshape(n, RET_QK_DIM // 2)
    log_gamma = jnp.log1p(-jnp.exp2(-5.0 - jnp.arange(RET_HEADS, dtype=F32)))

    xf, xb = ln_in(x.reshape(n, d), ln_in_g, ln_in_b)
    a_width = MLA_HEADS * MLA_V_DIM
    for l in range(depth):
        h = matmul(xb, _prep_w_in(w_in[l]), name="in_proj")
        q, kt, v1 = mla_proj(h, cos2, sin2, q_norm_g[l].reshape(1, -1), kv_norm_g[l].reshape(1, -1),
                             _prep_w_uq(w_uq[l]), _prep_w_ukv(w_ukv[l]), batch, seq)
        a = mla_attention(q, kt, v1)
        r = retention(h, log_gamma, cos_r, sin_r, ret_gn_g[l], ret_gn_b[l], batch, seq)
        wo = w_out[l].astype(BF16)
        xf, xb = out_proj_ln(a, r, wo[:a_width], wo[a_width:], xf, ln1_g[l], ln1_b[l], alpha)
        xf, xb = ffn_ln(xb, w_gate[l].astype(BF16), w_up[l].astype(BF16), w_down[l].astype(BF16),
                        xf, ln2_g[l], ln2_b[l], alpha)
    return xf.reshape(batch, seq, d)
```

```python
import functools

import jax
import jax.numpy as jnp
from jax import lax
from jax.experimental import pallas as pl
from jax.experimental.pallas import tpu as pltpu

F32 = jnp.float32
BF16 = jnp.bfloat16

D_MODEL = 2048
CHUNK = 64
MLA_HEADS = 8
MLA_Q_LORA = 512
MLA_KV_LORA = 256
MLA_NOPE_DIM = 128
MLA_ROPE_DIM = 64
MLA_V_DIM = 128
MLA_QK_DIM = MLA_NOPE_DIM + MLA_ROPE_DIM
RET_HEADS = 4
RET_QK_DIM = 256
RET_V_DIM = 256
ROPE_THETA = 10000.0
LN_EPS = 1e-5
RMS_EPS = 1e-6
GN_EPS = 1e-5

COL_CQ = 0
COL_CKV = 512
COL_KR = 768
COL_RQ = 1024
COL_RK = 2048
COL_RV = 3072
COL_RG = 4096
D_IN_PAD = 5120

VMEM_LIMIT = 56 * 1024 * 1024

MASK_VALUE = -1e30
LOG2E = 1.4426950408889634
FFN_SUB = 256
ATTN_TILE = 256
MLA_V_PAD = MLA_V_DIM + 16


def _params(*sem):
    return pltpu.CompilerParams(dimension_semantics=sem, vmem_limit_bytes=VMEM_LIMIT)


def _layer_norm(y, g, b):
    mu = jnp.mean(y, axis=-1, keepdims=True)
    d = y - mu
    var = jnp.mean(d * d, axis=-1, keepdims=True)
    return d * lax.rsqrt(var + LN_EPS) * g + b


def _ln_in_kernel(x_ref, g_ref, b_ref, of_ref, ob_ref):
    y = _layer_norm(x_ref[...], g_ref[...], b_ref[...])
    of_ref[...] = y
    ob_ref[...] = y.astype(BF16)


def ln_in(x, g, b, bm=512):
    n, d = x.shape
    row = pl.BlockSpec((bm, d), lambda i: (i, 0))
    vec = pl.BlockSpec((1, d), lambda i: (0, 0))
    return pl.pallas_call(
        _ln_in_kernel,
        grid=(n // bm,),
        in_specs=[row, vec, vec],
        out_specs=[row, row],
        out_shape=[jax.ShapeDtypeStruct((n, d), F32), jax.ShapeDtypeStruct((n, d), BF16)],
        compiler_params=_params("parallel"),
        name="ln_in",
    )(x, g.reshape(1, d), b.reshape(1, d))


def _mm_kernel(a_ref, w_ref, o_ref):
    o_ref[...] = jnp.dot(a_ref[...], w_ref[...], preferred_element_type=F32).astype(o_ref.dtype)


def matmul(a, w, bm=1024, bn=1024, name="matmul"):
    m, k = a.shape
    _, n = w.shape
    return pl.pallas_call(
        _mm_kernel,
        grid=(m // bm, n // bn),
        in_specs=[pl.BlockSpec((bm, k), lambda i, j: (i, 0)),
                  pl.BlockSpec((k, bn), lambda i, j: (0, j))],
        out_specs=pl.BlockSpec((bm, bn), lambda i, j: (i, j)),
        out_shape=jax.ShapeDtypeStruct((m, n), BF16),
        compiler_params=_params("parallel", "arbitrary"),
        name=name,
    )(a, w)


def _rms_norm(t, g):
    return t * lax.rsqrt(jnp.mean(t * t, axis=-1, keepdims=True) + RMS_EPS) * g


def _mla_proj_kernel(cq_ref, ckv_ref, kr_ref, cos_ref, sin_ref, cos_t_ref, sin_t_ref, qg_ref, kvg_ref,
                     wq_t_ref, wk_ref, wv_t_ref, q_t_ref, k_ref, v_t_ref, *, q_scale):
    cqn = _rms_norm(cq_ref[...].astype(F32), qg_ref[...])
    ckvn = _rms_norm(ckv_ref[...].astype(F32), kvg_ref[...])
    cqn_t = cqn.T.astype(BF16)
    ckvn_t = ckvn.T.astype(BF16)
    ckvn = ckvn.astype(BF16)
    kr = kr_ref[...].astype(F32)
    k_rope = (kr[:, :MLA_ROPE_DIM] * cos_ref[...] + kr[:, MLA_ROPE_DIM:] * sin_ref[...]).astype(BF16)
    cos_t = cos_t_ref[...]
    sin_t = sin_t_ref[...]
    ones = jnp.ones((MLA_V_PAD - MLA_V_DIM, cqn_t.shape[1]), BF16)
    for h in range(MLA_HEADS):
        qh = jnp.dot(wq_t_ref[h], cqn_t, preferred_element_type=F32)
        q_t_ref[0, h, :MLA_NOPE_DIM, :] = (qh[:MLA_NOPE_DIM] * q_scale).astype(BF16)
        q_rope = qh[MLA_NOPE_DIM:MLA_QK_DIM] * cos_t + qh[MLA_QK_DIM:] * sin_t
        q_t_ref[0, h, MLA_NOPE_DIM:, :] = (q_rope * q_scale).astype(BF16)
        k_ref[0, h, :, :MLA_NOPE_DIM] = jnp.dot(ckvn, wk_ref[h], preferred_element_type=F32).astype(BF16)
        k_ref[0, h, :, MLA_NOPE_DIM:] = k_rope
        v_t_ref[0, h, :MLA_V_DIM, :] = jnp.dot(wv_t_ref[h], ckvn_t, preferred_element_type=F32).astype(BF16)
        v_t_ref[0, h, MLA_V_DIM:, :] = ones


def mla_proj(h, cos2, sin2, cos2_t, sin2_t, qg, kvg, wq_t, wk, wv_t, batch, seq, bm=512):
    n = h.shape[0]
    spb = seq // bm
    H = MLA_HEADS
    q_scale = (MLA_QK_DIM ** -0.5) * LOG2E
    full = lambda shape: pl.BlockSpec(shape, lambda i: (0,) * len(shape))
    return pl.pallas_call(
        functools.partial(_mla_proj_kernel, q_scale=q_scale),
        grid=(n // bm,),
        in_specs=[
            pl.BlockSpec((bm, MLA_Q_LORA), lambda i: (i, COL_CQ // MLA_Q_LORA)),
            pl.BlockSpec((bm, MLA_KV_LORA), lambda i: (i, COL_CKV // MLA_KV_LORA)),
            pl.BlockSpec((bm, 2 * MLA_ROPE_DIM), lambda i: (i, COL_KR // (2 * MLA_ROPE_DIM))),
            pl.BlockSpec((bm, MLA_ROPE_DIM), lambda i: (i, 0)),
            pl.BlockSpec((bm, MLA_ROPE_DIM), lambda i: (i, 0)),
            pl.BlockSpec((MLA_ROPE_DIM, bm), lambda i: (0, i)),
            pl.BlockSpec((MLA_ROPE_DIM, bm), lambda i: (0, i)),
            full((1, MLA_Q_LORA)),
            full((1, MLA_KV_LORA)),
            full((H, 256, MLA_Q_LORA)),
            full((H, MLA_KV_LORA, MLA_NOPE_DIM)),
            full((H, MLA_V_DIM, MLA_KV_LORA)),
        ],
        out_specs=[
            pl.BlockSpec((1, H, MLA_QK_DIM, bm), lambda i: (i // spb, 0, 0, i % spb)),
            pl.BlockSpec((1, H, bm, MLA_QK_DIM), lambda i: (i // spb, 0, i % spb, 0)),
            pl.BlockSpec((1, H, MLA_V_PAD, bm), lambda i: (i // spb, 0, 0, i % spb)),
        ],
        out_shape=[
            jax.ShapeDtypeStruct((batch, H, MLA_QK_DIM, seq), BF16),
            jax.ShapeDtypeStruct((batch, H, seq, MLA_QK_DIM), BF16),
            jax.ShapeDtypeStruct((batch, H, MLA_V_PAD, seq), BF16),
        ],
        compiler_params=_params("parallel"),
        name="mla_proj",
    )(h, h, h, cos2, sin2, cos2_t, sin2_t, qg, kvg, wq_t, wk, wv_t)


def _attn_kernel(q_t_ref, k_ref, v_t_ref, o_ref, s_ref, p_ref, alpha_ref, m_ref, acc_ref, *, tq):
    qi = pl.program_id(2)
    n_rt = tq // ATTN_TILE
    n_full = qi * n_rt
    half = ATTN_TILE // 2
    m_ref[...] = jnp.full(m_ref.shape, MASK_VALUE, F32)
    acc_ref[...] = jnp.zeros(acc_ref.shape, F32)

    def key_start(tile):
        return pl.multiple_of(tile * ATTN_TILE, ATTN_TILE)

    def scores(tile, rt):
        k = k_ref[0, 0, pl.ds(key_start(tile), ATTN_TILE), :]
        return jnp.dot(k, q_t_ref[0, 0, :, rt * ATTN_TILE:(rt + 1) * ATTN_TILE],
                       preferred_element_type=F32)

    def load_scores(slot, rt, part, masked):
        s = s_ref[slot, rt, part * half:(part + 1) * half, :]
        if masked:
            kc = (lax.broadcasted_iota(jnp.int32, s.shape, 0) + part * half) // CHUNK
            qc = lax.broadcasted_iota(jnp.int32, s.shape, 1) // CHUNK
            s = jnp.where(qc >= kc, s, MASK_VALUE)
        return s

    def softmax(slot, rt, masked):
        m_old = m_ref[rt, 0:1, :]
        m_new = m_old
        for part in range(2):
            m_new = jnp.maximum(m_new, jnp.max(load_scores(slot, rt, part, masked), axis=0, keepdims=True))
        for part in range(2):
            p = jnp.exp2(load_scores(slot, rt, part, masked) - m_new)
            p_ref[slot, rt, part * half:(part + 1) * half, :] = p.astype(BF16)
        alpha_ref[slot, rt] = jnp.broadcast_to(jnp.exp2(m_old - m_new), alpha_ref.shape[2:])
        m_ref[rt] = jnp.broadcast_to(m_new, m_ref.shape[1:])

    def accumulate(tile, slot, rt):
        v_t = v_t_ref[0, 0, :, pl.ds(key_start(tile), ATTN_TILE)]
        pv = jnp.dot(v_t, p_ref[slot, rt], preferred_element_type=F32)
        acc_ref[rt] = alpha_ref[slot, rt, 0:1, :] * acc_ref[rt] + pv

    def step(tile, slot, diag):
        if diag is None:
            cur, nxt = range(n_rt), range(n_rt)
        else:
            cur, nxt = range(diag, n_rt), range(diag + 1, n_rt)
        for rt in nxt:
            s_ref[1 - slot, rt] = scores(tile + 1, rt)
        for rt in cur:
            softmax(slot, rt, rt == diag)
        for rt in cur:
            accumulate(tile, slot, rt)

    def step_pair(tile, diag):
        step(tile, 0, diag)
        step(tile + 1, 1, None if diag is None else diag + 1)

    for rt in range(n_rt):
        s_ref[0, rt] = scores(0, rt)

    def body(u, carry):
        t = 2 * u
        pl.when(t < n_full)(functools.partial(step_pair, t, None))
        for d in range(0, n_rt, 2):
            pl.when(t == n_full + d)(functools.partial(step_pair, t, d))
        return carry

    assert n_rt % 2 == 0
    lax.fori_loop(0, (n_full + n_rt) // 2, body, 0)
    for rt in range(n_rt):
        acc = acc_ref[rt]
        o_t = acc[:MLA_V_DIM] / acc[MLA_V_DIM:MLA_V_DIM + 1]
        o_ref[rt * ATTN_TILE:(rt + 1) * ATTN_TILE, :] = o_t.T.astype(BF16)


def mla_attention(q_t, k, v_t, tq=1024):
    batch, H, seq, _ = k.shape
    nq = seq // tq
    n_rt = tq // ATTN_TILE
    return pl.pallas_call(
        functools.partial(_attn_kernel, tq=tq),
        grid=(batch, H, nq),
        in_specs=[
            pl.BlockSpec((1, 1, MLA_QK_DIM, tq), lambda b, h, i: (b, h, 0, i)),
            pl.BlockSpec((1, 1, seq, MLA_QK_DIM), lambda b, h, i: (b, h, 0, 0)),
            pl.BlockSpec((1, 1, MLA_V_PAD, seq), lambda b, h, i: (b, h, 0, 0)),
        ],
        out_specs=pl.BlockSpec((tq, MLA_V_DIM), lambda b, h, i: (b * nq + i, h)),
        out_shape=jax.ShapeDtypeStruct((batch * seq, H * MLA_V_DIM), BF16),
        scratch_shapes=[
            pltpu.VMEM((2, n_rt, ATTN_TILE, ATTN_TILE), F32),
            pltpu.VMEM((2, n_rt, ATTN_TILE, ATTN_TILE), BF16),
            pltpu.VMEM((2, n_rt, 8, ATTN_TILE), F32),
            pltpu.VMEM((n_rt, 8, ATTN_TILE), F32),
            pltpu.VMEM((n_rt, MLA_V_PAD, ATTN_TILE), F32),
        ],
        compiler_params=_params("parallel", "parallel", "arbitrary"),
        name="mla_attention",
    )(q_t, k, v_t)


def _ret_kernel(lg_ref, rq_ref, rk_ref, rv_ref, rg_ref, cos_ref, sin_ref, gng_ref, gnb_ref,
                o_ref, state_ref, decay_ref, qdec_ref, kdec_ref, *, T):
    h = pl.program_id(1)
    c = pl.program_id(2)
    lg = lg_ref[h]
    half = RET_QK_DIM // 2

    @pl.when(c == 0)
    def _():
        state_ref[...] = jnp.zeros_like(state_ref)
        row = lax.broadcasted_iota(jnp.int32, (T, T), 0)
        col = lax.broadcasted_iota(jnp.int32, (T, T), 1)
        dist = jnp.abs(row - col).astype(F32)
        decay_ref[...] = jnp.where(row // CHUNK >= col // CHUNK, jnp.exp(lg * dist), 0.0)
        pos = lax.broadcasted_iota(jnp.int32, (T, 1), 0).astype(F32)
        qdec_ref[...] = jnp.exp(lg * (pos + 1.0))
        kdec_ref[...] = jnp.exp(lg * (T - 1.0 - pos))

    cos = cos_ref[...]
    sin = sin_ref[...]

    def rope(t):
        t1, t2 = t[:, :half], t[:, half:]
        return jnp.concatenate([t1 * cos - t2 * sin, t2 * cos + t1 * sin], axis=-1)

    q = rope(rq_ref[...].astype(F32)) * (RET_QK_DIM ** -0.5)
    k = rope(rk_ref[...].astype(F32))
    v = rv_ref[...]

    qb = q.astype(BF16)
    s = lax.dot_general(qb, k.astype(BF16), (((1,), (1,)), ((), ())), preferred_element_type=F32)
    o = jnp.dot((s * decay_ref[...]).astype(BF16), v, preferred_element_type=F32)
    state = state_ref[...]
    o = o + jnp.dot((q * qdec_ref[...]).astype(BF16), state.astype(BF16), preferred_element_type=F32)
    kv = lax.dot_general((k * kdec_ref[...]).astype(BF16), v, (((0,), (0,)), ((), ())),
                         preferred_element_type=F32)
    state_ref[...] = state * jnp.exp(lg * T) + kv

    mu = jnp.mean(o, axis=-1, keepdims=True)
    d = o - mu
    var = jnp.mean(d * d, axis=-1, keepdims=True)
    o = d * lax.rsqrt(var + GN_EPS) * gng_ref[...] + gnb_ref[...]
    g = rg_ref[...].astype(F32)
    o_ref[...] = (g * jax.nn.sigmoid(g) * o).astype(BF16)


def retention(h, log_gamma, cos_r, sin_r, gn_g, gn_b, batch, seq, T=512):
    n = h.shape[0]
    H = RET_HEADS
    nc = seq // T
    W = RET_QK_DIM

    def col(base):
        return pl.BlockSpec((T, W), lambda b, hh, c: (b * nc + c, base // W + hh))

    tab = pl.BlockSpec((T, W // 2), lambda b, hh, c: (b * nc + c, 0))
    vec = pl.BlockSpec((1, W), lambda b, hh, c: (0, hh))
    return pl.pallas_call(
        functools.partial(_ret_kernel, T=T),
        grid=(batch, H, nc),
        in_specs=[pl.BlockSpec(memory_space=pltpu.SMEM),
                  col(COL_RQ), col(COL_RK), col(COL_RV), col(COL_RG), tab, tab, vec, vec],
        out_specs=pl.BlockSpec((T, W), lambda b, hh, c: (b * nc + c, hh)),
        out_shape=jax.ShapeDtypeStruct((n, H * RET_V_DIM), BF16),
        scratch_shapes=[pltpu.VMEM((RET_QK_DIM, RET_V_DIM), F32), pltpu.VMEM((T, T), F32),
                        pltpu.VMEM((T, 1), F32), pltpu.VMEM((T, 1), F32)],
        compiler_params=_params("parallel", "parallel", "arbitrary"),
        name="retention",
    )(log_gamma, h, h, h, h, cos_r, sin_r, gn_g.reshape(1, -1), gn_b.reshape(1, -1))


def _out_ln_kernel(a_ref, r_ref, wa_ref, wr_ref, x_ref, g_ref, b_ref, of_ref, ob_ref, *, alpha):
    mix = jnp.dot(a_ref[...], wa_ref[...], preferred_element_type=F32)
    mix = mix + jnp.dot(r_ref[...], wr_ref[...], preferred_element_type=F32)
    y = _layer_norm(alpha * x_ref[...] + mix, g_ref[...], b_ref[...])
    of_ref[...] = y
    ob_ref[...] = y.astype(BF16)


def out_proj_ln(a, r, w_a, w_r, x, g, b, alpha, bm=512):
    n, d = x.shape
    ka, kr = a.shape[1], r.shape[1]
    row = pl.BlockSpec((bm, d), lambda i: (i, 0))
    vec = pl.BlockSpec((1, d), lambda i: (0, 0))
    return pl.pallas_call(
        functools.partial(_out_ln_kernel, alpha=alpha),
        grid=(n // bm,),
        in_specs=[pl.BlockSpec((bm, ka), lambda i: (i, 0)),
                  pl.BlockSpec((bm, kr), lambda i: (i, 0)),
                  pl.BlockSpec((ka, d), lambda i: (0, 0)),
                  pl.BlockSpec((kr, d), lambda i: (0, 0)),
                  row, vec, vec],
        out_specs=[row, row],
        out_shape=[jax.ShapeDtypeStruct((n, d), F32), jax.ShapeDtypeStruct((n, d), BF16)],
        compiler_params=_params("parallel"),
        name="out_proj_ln",
    )(a, r, w_a, w_r, x, g.reshape(1, d), b.reshape(1, d))


def _ffn_ln_kernel(xb_ref, wg_ref, wu_ref, wd_ref, x_ref, g_ref, b_ref, of_ref, ob_ref, acc_ref,
                   *, alpha):
    j = pl.program_id(1)

    @pl.when(j == 0)
    def _():
        acc_ref[...] = jnp.zeros_like(acc_ref)

    xb = xb_ref[...]
    part = None
    for s in range(wg_ref.shape[1] // FFN_SUB):
        cols = slice(s * FFN_SUB, (s + 1) * FFN_SUB)
        g = jnp.dot(xb, wg_ref[:, cols], preferred_element_type=F32)
        u = jnp.dot(xb, wu_ref[:, cols], preferred_element_type=F32)
        hm = (g * jax.nn.sigmoid(g) * u).astype(BF16)
        d = jnp.dot(hm, wd_ref[cols, :], preferred_element_type=F32)
        part = d if part is None else part + d
    acc_ref[...] += part

    @pl.when(j == pl.num_programs(1) - 1)
    def _():
        y = _layer_norm(alpha * x_ref[...] + acc_ref[...], g_ref[...], b_ref[...])
        of_ref[...] = y
        ob_ref[...] = y.astype(BF16)


def ffn_ln(xb, wg, wu, wd, x, g, b, alpha, bm=512, bn=512):
    n, d = x.shape
    dff = wg.shape[1]
    row = pl.BlockSpec((bm, d), lambda i, j: (i, 0))
    vec = pl.BlockSpec((1, d), lambda i, j: (0, 0))
    wcol = pl.BlockSpec((d, bn), lambda i, j: (0, j))
    return pl.pallas_call(
        functools.partial(_ffn_ln_kernel, alpha=alpha),
        grid=(n // bm, dff // bn),
        in_specs=[row, wcol, wcol, pl.BlockSpec((bn, d), lambda i, j: (j, 0)), row, vec, vec],
        out_specs=[row, row],
        out_shape=[jax.ShapeDtypeStruct((n, d), F32), jax.ShapeDtypeStruct((n, d), BF16)],
        scratch_shapes=[pltpu.VMEM((bm, d), F32)],
        compiler_params=_params("parallel", "arbitrary"),
        name="ffn_ln",
    )(xb, wg, wu, wd, x, g.reshape(1, d), b.reshape(1, d))


def _rotate_half_cols(w):
    half = w.shape[-1] // 2
    return jnp.concatenate([-w[..., half:], w[..., :half]], axis=-1)


def _prep_w_in(w):
    w = w.astype(BF16)
    kr = w[:, 768:832]
    pad = jnp.zeros((w.shape[0], COL_RQ - COL_KR - 2 * MLA_ROPE_DIM), w.dtype)
    return jnp.concatenate([w[:, :768], kr, _rotate_half_cols(kr), pad, w[:, 832:]], axis=1)


def _prep_w_uq(w):
    w = w.astype(BF16).reshape(MLA_Q_LORA, MLA_HEADS, MLA_QK_DIM)
    rope = w[..., MLA_NOPE_DIM:]
    w = jnp.concatenate([w, _rotate_half_cols(rope)], axis=-1)
    return w.transpose(1, 2, 0)


def _prep_w_ukv(w):
    w = w.astype(BF16).reshape(MLA_KV_LORA, MLA_HEADS, MLA_NOPE_DIM + MLA_V_DIM)
    return w[..., :MLA_NOPE_DIM].transpose(1, 0, 2), w[..., MLA_NOPE_DIM:].transpose(1, 2, 0)


def _rope_tables(positions, dim):
    inv_freq = ROPE_THETA ** (-jnp.arange(0, dim, 2, dtype=F32) / dim)
    ang = positions.astype(F32)[..., None] * inv_freq
    return jnp.cos(ang), jnp.sin(ang)


def kernel(x, positions, ln_in_g, ln_in_b, w_in, q_norm_g, kv_norm_g, w_uq, w_ukv, ret_gn_g, ret_gn_b, w_out, ln1_g, ln1_b, w_gate, w_up, w_down, ln2_g, ln2_b):
    batch, seq, d = x.shape
    depth = w_in.shape[0]
    n = batch * seq
    alpha = (2 * depth) ** 0.25

    cos_m, sin_m = _rope_tables(positions, MLA_ROPE_DIM)
    cos_r, sin_r = _rope_tables(positions, RET_QK_DIM)
    cos2 = jnp.concatenate([cos_m, cos_m], axis=-1).reshape(n, MLA_ROPE_DIM)
    sin2 = jnp.concatenate([sin_m, sin_m], axis=-1).reshape(n, MLA_ROPE_DIM)
    cos2_t, sin2_t = cos2.T, sin2.T
    cos_r = cos_r.reshape(n, RET_QK_DIM // 2)
    sin_r = sin_r.reshape(n, RET_QK_DIM // 2)
    log_gamma = jnp.log1p(-jnp.exp2(-5.0 - jnp.arange(RET_HEADS, dtype=F32)))

    xf, xb = ln_in(x.reshape(n, d), ln_in_g, ln_in_b)
    a_width = MLA_HEADS * MLA_V_DIM
    for l in range(depth):
        h = matmul(xb, _prep_w_in(w_in[l]), name="in_proj")
        wk, wv_t = _prep_w_ukv(w_ukv[l])
        q_t, k, v_t = mla_proj(h, cos2, sin2, cos2_t, sin2_t, q_norm_g[l].reshape(1, -1),
                               kv_norm_g[l].reshape(1, -1), _prep_w_uq(w_uq[l]), wk, wv_t, batch, seq)
        a = mla_attention(q_t, k, v_t)
        r = retention(h, log_gamma, cos_r, sin_r, ret_gn_g[l], ret_gn_b[l], batch, seq)
        wo = w_out[l].astype(BF16)
        xf, xb = out_proj_ln(a, r, wo[:a_width], wo[a_width:], xf, ln1_g[l], ln1_b[l], alpha)
        xf, xb = ffn_ln(xb, w_gate[l].astype(BF16), w_up[l].astype(BF16), w_down[l].astype(BF16),
                        xf, ln2_g[l], ln2_b[l], alpha)
    return xf.reshape(batch, seq, d)
```

```python
import functools

import jax
import jax.numpy as jnp
from jax import lax
from jax.experimental import pallas as pl
from jax.experimental.pallas import tpu as pltpu

F32 = jnp.float32
BF16 = jnp.bfloat16

D_MODEL = 2048
CHUNK = 64
MLA_HEADS = 8
MLA_Q_LORA = 512
MLA_KV_LORA = 256
MLA_NOPE_DIM = 128
MLA_ROPE_DIM = 64
MLA_V_DIM = 128
MLA_QK_DIM = MLA_NOPE_DIM + MLA_ROPE_DIM
RET_HEADS = 4
RET_QK_DIM = 256
RET_V_DIM = 256
ROPE_THETA = 10000.0
LN_EPS = 1e-5
RMS_EPS = 1e-6
GN_EPS = 1e-5

COL_CQ = 0
COL_CKV = 512
COL_KR = 768
COL_RQ = 1024
COL_RK = 2048
COL_RV = 3072
COL_RG = 4096
D_IN_PAD = 5120

VMEM_LIMIT = 60 * 1024 * 1024

MASK_VALUE = -1e30
LOG2E = 1.4426950408889634
FFN_SUB = 256
FFN_OUT_CHUNK = 512
OUT_SUB = 256
ATTN_TILE = 256
MLA_V_PAD = MLA_V_DIM + 16


def _params(*sem):
    return pltpu.CompilerParams(dimension_semantics=sem, vmem_limit_bytes=VMEM_LIMIT)


def _layer_norm(y, g, b):
    mu = jnp.mean(y, axis=-1, keepdims=True)
    d = y - mu
    var = jnp.mean(d * d, axis=-1, keepdims=True)
    return d * lax.rsqrt(var + LN_EPS) * g + b


def _ln_in_kernel(x_ref, g_ref, b_ref, o_ref):
    o_ref[...] = _layer_norm(x_ref[...], g_ref[...], b_ref[...])


def ln_in(x, g, b, bm=512):
    n, d = x.shape
    row = pl.BlockSpec((bm, d), lambda i: (i, 0))
    vec = pl.BlockSpec((1, d), lambda i: (0, 0))
    return pl.pallas_call(
        _ln_in_kernel,
        grid=(n // bm,),
        in_specs=[row, vec, vec],
        out_specs=row,
        out_shape=jax.ShapeDtypeStruct((n, d), F32),
        compiler_params=_params("parallel"),
        name="ln_in",
    )(x, g.reshape(1, d), b.reshape(1, d))


def _in_proj_kernel(x_ref, w_ref, o_ref, xb_ref):
    @pl.when(pl.program_id(1) == 0)
    def _():
        xb_ref[...] = x_ref[...].astype(BF16)

    o_ref[...] = jnp.dot(xb_ref[...], w_ref[...], preferred_element_type=F32).astype(o_ref.dtype)


def in_proj(x, w, layer, bm=1024, bn=1024):
    m, k = x.shape
    n = w.shape[2]
    return pl.pallas_call(
        _in_proj_kernel,
        grid=(m // bm, n // bn),
        in_specs=[pl.BlockSpec((bm, k), lambda i, j: (i, 0)),
                  pl.BlockSpec((None, k, bn), lambda i, j: (layer, 0, j))],
        out_specs=pl.BlockSpec((bm, bn), lambda i, j: (i, j)),
        out_shape=jax.ShapeDtypeStruct((m, n), BF16),
        scratch_shapes=[pltpu.VMEM((bm, k), BF16)],
        compiler_params=_params("parallel", "arbitrary"),
        name="in_proj",
    )(x, w)


def _rms_norm(t, g):
    return t * lax.rsqrt(jnp.mean(t * t, axis=-1, keepdims=True) + RMS_EPS) * g


def _mla_proj_kernel(cq_ref, ckv_ref, kr_ref, cos_ref, sin_ref, cos_t_ref, sin_t_ref, qg_ref, kvg_ref,
                     wq_t_ref, wk_ref, wv_t_ref, q_t_ref, k_ref, v_t_ref, *, q_scale):
    cqn = _rms_norm(cq_ref[...].astype(F32), qg_ref[...])
    ckvn = _rms_norm(ckv_ref[...].astype(F32), kvg_ref[...])
    cqn_t = cqn.T.astype(BF16)
    ckvn_t = ckvn.T.astype(BF16)
    ckvn = ckvn.astype(BF16)
    kr = kr_ref[...].astype(F32)
    k_rope = (kr[:, :MLA_ROPE_DIM] * cos_ref[...] + kr[:, MLA_ROPE_DIM:] * sin_ref[...]).astype(BF16)
    cos_t = cos_t_ref[...]
    sin_t = sin_t_ref[...]
    ones = jnp.ones((MLA_V_PAD - MLA_V_DIM, cqn_t.shape[1]), BF16)
    for h in range(MLA_HEADS):
        qh = jnp.dot(wq_t_ref[h], cqn_t, preferred_element_type=F32)
        q_t_ref[0, h, :MLA_NOPE_DIM, :] = (qh[:MLA_NOPE_DIM] * q_scale).astype(BF16)
        q_rope = qh[MLA_NOPE_DIM:MLA_QK_DIM] * cos_t + qh[MLA_QK_DIM:] * sin_t
        q_t_ref[0, h, MLA_NOPE_DIM:, :] = (q_rope * q_scale).astype(BF16)
        k_ref[0, h, :, :MLA_NOPE_DIM] = jnp.dot(ckvn, wk_ref[h], preferred_element_type=F32).astype(BF16)
        k_ref[0, h, :, MLA_NOPE_DIM:] = k_rope
        v_t_ref[0, h, :MLA_V_DIM, :] = jnp.dot(wv_t_ref[h], ckvn_t, preferred_element_type=F32).astype(BF16)
        v_t_ref[0, h, MLA_V_DIM:, :] = ones


def mla_proj(h, cos2, sin2, cos2_t, sin2_t, qg, kvg, wq_t, wk, wv_t, batch, seq, bm=512):
    n = h.shape[0]
    spb = seq // bm
    H = MLA_HEADS
    q_scale = (MLA_QK_DIM ** -0.5) * LOG2E
    full = lambda shape: pl.BlockSpec(shape, lambda i: (0,) * len(shape))
    return pl.pallas_call(
        functools.partial(_mla_proj_kernel, q_scale=q_scale),
        grid=(n // bm,),
        in_specs=[
            pl.BlockSpec((bm, MLA_Q_LORA), lambda i: (i, COL_CQ // MLA_Q_LORA)),
            pl.BlockSpec((bm, MLA_KV_LORA), lambda i: (i, COL_CKV // MLA_KV_LORA)),
            pl.BlockSpec((bm, 2 * MLA_ROPE_DIM), lambda i: (i, COL_KR // (2 * MLA_ROPE_DIM))),
            pl.BlockSpec((bm, MLA_ROPE_DIM), lambda i: (i, 0)),
            pl.BlockSpec((bm, MLA_ROPE_DIM), lambda i: (i, 0)),
            pl.BlockSpec((MLA_ROPE_DIM, bm), lambda i: (0, i)),
            pl.BlockSpec((MLA_ROPE_DIM, bm), lambda i: (0, i)),
            full((1, MLA_Q_LORA)),
            full((1, MLA_KV_LORA)),
            full((H, 256, MLA_Q_LORA)),
            full((H, MLA_KV_LORA, MLA_NOPE_DIM)),
            full((H, MLA_V_DIM, MLA_KV_LORA)),
        ],
        out_specs=[
            pl.BlockSpec((1, H, MLA_QK_DIM, bm), lambda i: (i // spb, 0, 0, i % spb)),
            pl.BlockSpec((1, H, bm, MLA_QK_DIM), lambda i: (i // spb, 0, i % spb, 0)),
            pl.BlockSpec((1, H, MLA_V_PAD, bm), lambda i: (i // spb, 0, 0, i % spb)),
        ],
        out_shape=[
            jax.ShapeDtypeStruct((batch, H, MLA_QK_DIM, seq), BF16),
            jax.ShapeDtypeStruct((batch, H, seq, MLA_QK_DIM), BF16),
            jax.ShapeDtypeStruct((batch, H, MLA_V_PAD, seq), BF16),
        ],
        compiler_params=_params("parallel"),
        name="mla_proj",
    )(h, h, h, cos2, sin2, cos2_t, sin2_t, qg, kvg, wq_t, wk, wv_t)


def _attn_kernel(q_t_ref, k_ref, v_t_ref, o_ref, s_ref, p_ref, alpha_ref, m_ref, acc_ref, *, tq):
    qi = pl.program_id(2)
    n_rt = tq // ATTN_TILE
    n_full = qi * n_rt
    half = ATTN_TILE // 2
    m_ref[...] = jnp.full(m_ref.shape, MASK_VALUE, F32)
    acc_ref[...] = jnp.zeros(acc_ref.shape, F32)

    def key_start(tile):
        return pl.multiple_of(tile * ATTN_TILE, ATTN_TILE)

    def scores(tile, rt):
        k = k_ref[0, 0, pl.ds(key_start(tile), ATTN_TILE), :]
        return jnp.dot(k, q_t_ref[0, 0, :, rt * ATTN_TILE:(rt + 1) * ATTN_TILE],
                       preferred_element_type=F32)

    def load_scores(slot, rt, part, masked):
        s = s_ref[slot, rt, part * half:(part + 1) * half, :]
        if masked:
            kc = (lax.broadcasted_iota(jnp.int32, s.shape, 0) + part * half) // CHUNK
            qc = lax.broadcasted_iota(jnp.int32, s.shape, 1) // CHUNK
            s = jnp.where(qc >= kc, s, MASK_VALUE)
        return s

    def softmax(slot, rt, masked):
        m_old = m_ref[rt, 0:1, :]
        m_new = m_old
        for part in range(2):
            m_new = jnp.maximum(m_new, jnp.max(load_scores(slot, rt, part, masked), axis=0, keepdims=True))
        for part in range(2):
            p = jnp.exp2(load_scores(slot, rt, part, masked) - m_new)
            p_ref[slot, rt, part * half:(part + 1) * half, :] = p.astype(BF16)
        alpha_ref[slot, rt] = jnp.broadcast_to(jnp.exp2(m_old - m_new), alpha_ref.shape[2:])
        m_ref[rt] = jnp.broadcast_to(m_new, m_ref.shape[1:])

    def accumulate(tile, slot, rt):
        v_t = v_t_ref[0, 0, :, pl.ds(key_start(tile), ATTN_TILE)]
        pv = jnp.dot(v_t, p_ref[slot, rt], preferred_element_type=F32)
        acc_ref[rt] = alpha_ref[slot, rt, 0:1, :] * acc_ref[rt] + pv

    def step(tile, slot, diag):
        if diag is None:
            cur, nxt = range(n_rt), range(n_rt)
        else:
            cur, nxt = range(diag, n_rt), range(diag + 1, n_rt)
        for rt in nxt:
            s_ref[1 - slot, rt] = scores(tile + 1, rt)
        for rt in cur:
            softmax(slot, rt, rt == diag)
        for rt in cur:
            accumulate(tile, slot, rt)

    def step_pair(tile, diag):
        step(tile, 0, diag)
        step(tile + 1, 1, None if diag is None else diag + 1)

    for rt in range(n_rt):
        s_ref[0, rt] = scores(0, rt)

    def body(u, carry):
        t = 2 * u
        pl.when(t < n_full)(functools.partial(step_pair, t, None))
        for d in range(0, n_rt, 2):
            pl.when(t == n_full + d)(functools.partial(step_pair, t, d))
        return carry

    assert n_rt % 2 == 0
    lax.fori_loop(0, (n_full + n_rt) // 2, body, 0)
    for rt in range(n_rt):
        acc = acc_ref[rt]
        o_t = acc[:MLA_V_DIM] / acc[MLA_V_DIM:MLA_V_DIM + 1]
        o_ref[rt * ATTN_TILE:(rt + 1) * ATTN_TILE, :] = o_t.T.astype(BF16)


def mla_attention(q_t, k, v_t, tq=1024):
    batch, H, seq, _ = k.shape
    nq = seq // tq
    n_rt = tq // ATTN_TILE
    return pl.pallas_call(
        functools.partial(_attn_kernel, tq=tq),
        grid=(batch, H, nq),
        in_specs=[
            pl.BlockSpec((1, 1, MLA_QK_DIM, tq), lambda b, h, i: (b, h, 0, i)),
            pl.BlockSpec((1, 1, seq, MLA_QK_DIM), lambda b, h, i: (b, h, 0, 0)),
            pl.BlockSpec((1, 1, MLA_V_PAD, seq), lambda b, h, i: (b, h, 0, 0)),
        ],
        out_specs=pl.BlockSpec((tq, MLA_V_DIM), lambda b, h, i: (b * nq + i, h)),
        out_shape=jax.ShapeDtypeStruct((batch * seq, H * MLA_V_DIM), BF16),
        scratch_shapes=[
            pltpu.VMEM((2, n_rt, ATTN_TILE, ATTN_TILE), F32),
            pltpu.VMEM((2, n_rt, ATTN_TILE, ATTN_TILE), BF16),
            pltpu.VMEM((2, n_rt, 8, ATTN_TILE), F32),
            pltpu.VMEM((n_rt, 8, ATTN_TILE), F32),
            pltpu.VMEM((n_rt, MLA_V_PAD, ATTN_TILE), F32),
        ],
        compiler_params=_params("parallel", "parallel", "arbitrary"),
        name="mla_attention",
    )(q_t, k, v_t)


def _ret_kernel(lg_ref, rq_ref, rk_ref, rv_ref, rg_ref, cos_ref, sin_ref, gng_ref, gnb_ref,
                o_ref, state_ref, decay_ref, qdec_ref, kdec_ref, *, T):
    h = pl.program_id(1)
    c = pl.program_id(2)
    lg = lg_ref[h]
    half = RET_QK_DIM // 2

    @pl.when(c == 0)
    def _():
        state_ref[...] = jnp.zeros_like(state_ref)
        row = lax.broadcasted_iota(jnp.int32, (T, T), 0)
        col = lax.broadcasted_iota(jnp.int32, (T, T), 1)
        dist = jnp.abs(row - col).astype(F32)
        decay_ref[...] = jnp.where(row // CHUNK >= col // CHUNK, jnp.exp(lg * dist), 0.0)
        pos = lax.broadcasted_iota(jnp.int32, (T, 1), 0).astype(F32)
        qdec_ref[...] = jnp.exp(lg * (pos + 1.0))
        kdec_ref[...] = jnp.exp(lg * (T - 1.0 - pos))

    cos = cos_ref[...]
    sin = sin_ref[...]

    def rope(t):
        t1, t2 = t[:, :half], t[:, half:]
        return jnp.concatenate([t1 * cos - t2 * sin, t2 * cos + t1 * sin], axis=-1)

    q = rope(rq_ref[...].astype(F32)) * (RET_QK_DIM ** -0.5)
    k = rope(rk_ref[...].astype(F32))
    v = rv_ref[...]

    qb = q.astype(BF16)
    s = lax.dot_general(qb, k.astype(BF16), (((1,), (1,)), ((), ())), preferred_element_type=F32)
    o = jnp.dot((s * decay_ref[...]).astype(BF16), v, preferred_element_type=F32)
    state = state_ref[...]
    o = o + jnp.dot((q * qdec_ref[...]).astype(BF16), state.astype(BF16), preferred_element_type=F32)
    kv = lax.dot_general((k * kdec_ref[...]).astype(BF16), v, (((0,), (0,)), ((), ())),
                         preferred_element_type=F32)
    state_ref[...] = state * jnp.exp(lg * T) + kv

    mu = jnp.mean(o, axis=-1, keepdims=True)
    d = o - mu
    var = jnp.mean(d * d, axis=-1, keepdims=True)
    o = d * lax.rsqrt(var + GN_EPS) * gng_ref[...] + gnb_ref[...]
    g = rg_ref[...].astype(F32)
    o_ref[...] = (g * jax.nn.sigmoid(g) * o).astype(BF16)


def retention(h, log_gamma, cos_r, sin_r, gn_g, gn_b, batch, seq, T=512):
    n = h.shape[0]
    H = RET_HEADS
    nc = seq // T
    W = RET_QK_DIM

    def col(base):
        return pl.BlockSpec((T, W), lambda b, hh, c: (b * nc + c, base // W + hh))

    tab = pl.BlockSpec((T, W // 2), lambda b, hh, c: (b * nc + c, 0))
    vec = pl.BlockSpec((1, W), lambda b, hh, c: (0, hh))
    return pl.pallas_call(
        functools.partial(_ret_kernel, T=T),
        grid=(batch, H, nc),
        in_specs=[pl.BlockSpec(memory_space=pltpu.SMEM),
                  col(COL_RQ), col(COL_RK), col(COL_RV), col(COL_RG), tab, tab, vec, vec],
        out_specs=pl.BlockSpec((T, W), lambda b, hh, c: (b * nc + c, hh)),
        out_shape=jax.ShapeDtypeStruct((n, H * RET_V_DIM), BF16),
        scratch_shapes=[pltpu.VMEM((RET_QK_DIM, RET_V_DIM), F32), pltpu.VMEM((T, T), F32),
                        pltpu.VMEM((T, 1), F32), pltpu.VMEM((T, 1), F32)],
        compiler_params=_params("parallel", "parallel", "arbitrary"),
        name="retention",
    )(log_gamma, h, h, h, h, cos_r, sin_r, gn_g.reshape(1, -1), gn_b.reshape(1, -1))


def _out_ln_kernel(a_ref, r_ref, wa_ref, wr_ref, x_ref, g_ref, b_ref, o_ref, *, alpha):
    for s in range(a_ref.shape[0] // OUT_SUB):
        rows = slice(s * OUT_SUB, (s + 1) * OUT_SUB)
        mix = jnp.dot(a_ref[rows, :], wa_ref[...], preferred_element_type=F32)
        mix = mix + jnp.dot(r_ref[rows, :], wr_ref[...], preferred_element_type=F32)
        o_ref[rows, :] = _layer_norm(alpha * x_ref[rows, :] + mix, g_ref[...], b_ref[...])


def out_proj_ln(a, r, w, layer, x, g, b, alpha, bm=512):
    n, d = x.shape
    ka, kr = a.shape[1], r.shape[1]
    assert ka == kr
    row = pl.BlockSpec((bm, d), lambda i: (i, 0))
    vec = pl.BlockSpec((1, d), lambda i: (0, 0))
    return pl.pallas_call(
        functools.partial(_out_ln_kernel, alpha=alpha),
        grid=(n // bm,),
        in_specs=[pl.BlockSpec((bm, ka), lambda i: (i, 0)),
                  pl.BlockSpec((bm, kr), lambda i: (i, 0)),
                  pl.BlockSpec((None, ka, d), lambda i: (layer, 0, 0)),
                  pl.BlockSpec((None, kr, d), lambda i: (layer, 1, 0)),
                  row, vec, vec],
        out_specs=row,
        out_shape=jax.ShapeDtypeStruct((n, d), F32),
        compiler_params=_params("parallel"),
        name="out_proj_ln",
    )(a, r, w, w, x, g.reshape(1, d), b.reshape(1, d))


def _ffn_ln_kernel(x_ref, wg_ref, wu_ref, wd_ref, g_ref, b_ref, o_ref, xb_ref, *, alpha):
    j = pl.program_id(1)

    @pl.when(j == 0)
    def _():
        xb_ref[...] = x_ref[...].astype(BF16)
        o_ref[...] = jnp.zeros_like(o_ref)

    xb = xb_ref[...]
    hidden = []
    for s in range(wg_ref.shape[1] // FFN_SUB):
        cols = slice(s * FFN_SUB, (s + 1) * FFN_SUB)
        g = jnp.dot(xb, wg_ref[:, cols], preferred_element_type=F32)
        u = jnp.dot(xb, wu_ref[:, cols], preferred_element_type=F32)
        hidden.append((g * jax.nn.sigmoid(g) * u).astype(BF16))
    hm = jnp.concatenate(hidden, axis=1)
    for c in range(o_ref.shape[1] // FFN_OUT_CHUNK):
        cols = slice(c * FFN_OUT_CHUNK, (c + 1) * FFN_OUT_CHUNK)
        o_ref[:, cols] += jnp.dot(hm, wd_ref[:, cols], preferred_element_type=F32)

    @pl.when(j == pl.num_programs(1) - 1)
    def _():
        o_ref[...] = _layer_norm(alpha * x_ref[...] + o_ref[...], g_ref[...], b_ref[...])


def ffn_ln(x, wg, wu, wd, layer, g, b, alpha, bm=1024, bn=256):
    n, d = x.shape
    dff = wg.shape[2]
    row = pl.BlockSpec((bm, d), lambda i, j: (i, 0))
    vec = pl.BlockSpec((1, d), lambda i, j: (0, 0))
    wcol = pl.BlockSpec((None, d, bn), lambda i, j: (layer, 0, j))
    return pl.pallas_call(
        functools.partial(_ffn_ln_kernel, alpha=alpha),
        grid=(n // bm, dff // bn),
        in_specs=[row, wcol, wcol, pl.BlockSpec((None, bn, d), lambda i, j: (layer, j, 0)), vec, vec],
        out_specs=row,
        out_shape=jax.ShapeDtypeStruct((n, d), F32),
        scratch_shapes=[pltpu.VMEM((bm, d), BF16)],
        compiler_params=_params("parallel", "arbitrary"),
        name="ffn_ln",
    )(x, wg, wu, wd, g.reshape(1, d), b.reshape(1, d))


def _rotate_half_cols(w):
    half = w.shape[-1] // 2
    return jnp.concatenate([-w[..., half:], w[..., :half]], axis=-1)


def _prep_w_in(w):
    w = w.astype(BF16)
    kr = w[..., 768:832]
    pad = jnp.zeros(w.shape[:-1] + (COL_RQ - COL_KR - 2 * MLA_ROPE_DIM,), w.dtype)
    return jnp.concatenate([w[..., :768], kr, _rotate_half_cols(kr), pad, w[..., 832:]], axis=-1)


def _prep_w_uq(w):
    w = w.astype(BF16).reshape(MLA_Q_LORA, MLA_HEADS, MLA_QK_DIM)
    rope = w[..., MLA_NOPE_DIM:]
    w = jnp.concatenate([w, _rotate_half_cols(rope)], axis=-1)
    return w.transpose(1, 2, 0)


def _prep_w_ukv(w):
    w = w.astype(BF16).reshape(MLA_KV_LORA, MLA_HEADS, MLA_NOPE_DIM + MLA_V_DIM)
    return w[..., :MLA_NOPE_DIM].transpose(1, 0, 2), w[..., MLA_NOPE_DIM:].transpose(1, 2, 0)


def _rope_tables(positions, dim):
    inv_freq = ROPE_THETA ** (-jnp.arange(0, dim, 2, dtype=F32) / dim)
    ang = positions.astype(F32)[..., None] * inv_freq
    return jnp.cos(ang), jnp.sin(ang)


def kernel(x, positions, ln_in_g, ln_in_b, w_in, q_norm_g, kv_norm_g, w_uq, w_ukv, ret_gn_g, ret_gn_b, w_out, ln1_g, ln1_b, w_gate, w_up, w_down, ln2_g, ln2_b):
    batch, seq, d = x.shape
    depth = w_in.shape[0]
    n = batch * seq
    alpha = (2 * depth) ** 0.25

    cos_m, sin_m = _rope_tables(positions, MLA_ROPE_DIM)
    cos_r, sin_r = _rope_tables(positions, RET_QK_DIM)
    cos2 = jnp.concatenate([cos_m, cos_m], axis=-1).reshape(n, MLA_ROPE_DIM)
    sin2 = jnp.concatenate([sin_m, sin_m], axis=-1).reshape(n, MLA_ROPE_DIM)
    cos2_t, sin2_t = cos2.T, sin2.T
    cos_r = cos_r.reshape(n, RET_QK_DIM // 2)
    sin_r = sin_r.reshape(n, RET_QK_DIM // 2)
    log_gamma = jnp.log1p(-jnp.exp2(-5.0 - jnp.arange(RET_HEADS, dtype=F32)))

    w_in_b = _prep_w_in(w_in)
    w_out_b = w_out.astype(BF16)
    w_gate_b, w_up_b, w_down_b = w_gate.astype(BF16), w_up.astype(BF16), w_down.astype(BF16)

    xf = ln_in(x.reshape(n, d), ln_in_g, ln_in_b)
    for l in range(depth):
        h = in_proj(xf, w_in_b, l)
        wk, wv_t = _prep_w_ukv(w_ukv[l])
        q_t, k, v_t = mla_proj(h, cos2, sin2, cos2_t, sin2_t, q_norm_g[l].reshape(1, -1),
                               kv_norm_g[l].reshape(1, -1), _prep_w_uq(w_uq[l]), wk, wv_t, batch, seq)
        a = mla_attention(q_t, k, v_t)
        r = retention(h, log_gamma, cos_r, sin_r, ret_gn_g[l], ret_gn_b[l], batch, seq)
        xf = out_proj_ln(a, r, w_out_b, l, xf, ln1_g[l], ln1_b[l], alpha)
        xf = ffn_ln(xf, w_gate_b, w_up_b, w_down_b, l, ln2_g[l], ln2_b[l], alpha)
    return xf.reshape(batch, seq, d)
```

```python
import functools

import jax
import jax.numpy as jnp
from jax import lax
from jax.experimental import pallas as pl
from jax.experimental.pallas import tpu as pltpu

F32 = jnp.float32
BF16 = jnp.bfloat16

D_MODEL = 2048
CHUNK = 64
MLA_HEADS = 8
MLA_Q_LORA = 512
MLA_KV_LORA = 256
MLA_NOPE_DIM = 128
MLA_ROPE_DIM = 64
MLA_V_DIM = 128
MLA_QK_DIM = MLA_NOPE_DIM + MLA_ROPE_DIM
RET_HEADS = 4
RET_QK_DIM = 256
RET_V_DIM = 256
ROPE_THETA = 10000.0
LN_EPS = 1e-5
RMS_EPS = 1e-6
GN_EPS = 1e-5

COL_CQ = 0
COL_CKV = 512
COL_KR = 768
MLA_IN_WIDTH = 1024
COL_RQ = 0
COL_RK = 1024
COL_RV = 2048
COL_RG = 3072
W_IN_MLA_COLS = 832

VMEM_LIMIT = 60 * 1024 * 1024

MASK_VALUE = -1e30
LOG2E = 1.4426950408889634
FFN_SUB = 256
FFN_OUT_CHUNK = 512
OUT_SUB = 256
ATTN_TILE = 256
ATTN_GROUP = 4
MLA_V_PAD = MLA_V_DIM + 16


def _params(*sem):
    return pltpu.CompilerParams(dimension_semantics=sem, vmem_limit_bytes=VMEM_LIMIT)


def _layer_norm(y, g, b):
    mu = jnp.mean(y, axis=-1, keepdims=True)
    d = y - mu
    var = jnp.mean(d * d, axis=-1, keepdims=True)
    return d * lax.rsqrt(var + LN_EPS) * g + b


def _ln_in_kernel(x_ref, g_ref, b_ref, o_ref):
    o_ref[...] = _layer_norm(x_ref[...], g_ref[...], b_ref[...])


def ln_in(x, g, b, bm=512):
    n, d = x.shape
    row = pl.BlockSpec((bm, d), lambda i: (i, 0))
    vec = pl.BlockSpec((1, d), lambda i: (0, 0))
    return pl.pallas_call(
        _ln_in_kernel,
        grid=(n // bm,),
        in_specs=[row, vec, vec],
        out_specs=row,
        out_shape=jax.ShapeDtypeStruct((n, d), F32),
        compiler_params=_params("parallel"),
        name="ln_in",
    )(x, g.reshape(1, d), b.reshape(1, d))


def _in_proj_kernel(x_ref, wm_ref, wr_ref, om_ref, or_ref, xb_ref):
    j = pl.program_id(1)

    @pl.when(j == 0)
    def _():
        xb_ref[...] = x_ref[...].astype(BF16)
        om_ref[...] = jnp.dot(xb_ref[...], wm_ref[...], preferred_element_type=F32).astype(BF16)

    @pl.when(j > 0)
    def _():
        or_ref[...] = jnp.dot(xb_ref[...], wr_ref[...], preferred_element_type=F32).astype(BF16)


def in_proj(x, w_mla, w_ret, layer, bm=1024):
    m, k = x.shape
    bn = w_mla.shape[2]
    n_ret = w_ret.shape[2]
    return pl.pallas_call(
        _in_proj_kernel,
        grid=(m // bm, 1 + n_ret // bn),
        in_specs=[pl.BlockSpec((bm, k), lambda i, j: (i, 0)),
                  pl.BlockSpec((None, k, bn), lambda i, j: (layer, 0, 0)),
                  pl.BlockSpec((None, k, bn), lambda i, j: (layer, 0, jnp.maximum(j - 1, 0)))],
        out_specs=[pl.BlockSpec((bm, bn), lambda i, j: (i, 0)),
                   pl.BlockSpec((bm, bn), lambda i, j: (i, jnp.maximum(j - 1, 0)))],
        out_shape=[jax.ShapeDtypeStruct((m, bn), BF16), jax.ShapeDtypeStruct((m, n_ret), BF16)],
        scratch_shapes=[pltpu.VMEM((bm, k), BF16)],
        compiler_params=_params("parallel", "arbitrary"),
        name="in_proj",
    )(x, w_mla, w_ret)


def _rms_norm(t, g):
    return t * lax.rsqrt(jnp.mean(t * t, axis=-1, keepdims=True) + RMS_EPS) * g


def _mla_proj_kernel(cq_ref, ckv_ref, kr_ref, cos_ref, sin_ref, cos_t_ref, sin_t_ref, qg_ref, kvg_ref,
                     wq_t_ref, wk_ref, wv_t_ref, q_t_ref, k_ref, v_t_ref, *, q_scale):
    cqn = _rms_norm(cq_ref[...].astype(F32), qg_ref[...])
    ckvn = _rms_norm(ckv_ref[...].astype(F32), kvg_ref[...])
    cqn_t = cqn.T.astype(BF16)
    ckvn_t = ckvn.T.astype(BF16)
    ckvn = ckvn.astype(BF16)
    kr = kr_ref[...].astype(F32)
    k_rope = (kr[:, :MLA_ROPE_DIM] * cos_ref[...] + kr[:, MLA_ROPE_DIM:] * sin_ref[...]).astype(BF16)
    cos_t = cos_t_ref[...]
    sin_t = sin_t_ref[...]
    ones = jnp.ones((MLA_V_PAD - MLA_V_DIM, cqn_t.shape[1]), BF16)
    for h in range(MLA_HEADS):
        qh = jnp.dot(wq_t_ref[h], cqn_t, preferred_element_type=F32)
        q_t_ref[0, h, :MLA_NOPE_DIM, :] = (qh[:MLA_NOPE_DIM] * q_scale).astype(BF16)
        q_rope = qh[MLA_NOPE_DIM:MLA_QK_DIM] * cos_t + qh[MLA_QK_DIM:] * sin_t
        q_t_ref[0, h, MLA_NOPE_DIM:, :] = (q_rope * q_scale).astype(BF16)
        k_ref[0, h, :, :MLA_NOPE_DIM] = jnp.dot(ckvn, wk_ref[h], preferred_element_type=F32).astype(BF16)
        k_ref[0, h, :, MLA_NOPE_DIM:] = k_rope
        v_t_ref[0, h, :MLA_V_DIM, :] = jnp.dot(wv_t_ref[h], ckvn_t, preferred_element_type=F32).astype(BF16)
        v_t_ref[0, h, MLA_V_DIM:, :] = ones


def mla_proj(h, cos2, sin2, cos2_t, sin2_t, qg, kvg, wq_t, wk, wv_t, batch, seq, bm=512):
    n = h.shape[0]
    spb = seq // bm
    H = MLA_HEADS
    q_scale = (MLA_QK_DIM ** -0.5) * LOG2E
    full = lambda shape: pl.BlockSpec(shape, lambda i: (0,) * len(shape))
    return pl.pallas_call(
        functools.partial(_mla_proj_kernel, q_scale=q_scale),
        grid=(n // bm,),
        in_specs=[
            pl.BlockSpec((bm, MLA_Q_LORA), lambda i: (i, COL_CQ // MLA_Q_LORA)),
            pl.BlockSpec((bm, MLA_KV_LORA), lambda i: (i, COL_CKV // MLA_KV_LORA)),
            pl.BlockSpec((bm, 2 * MLA_ROPE_DIM), lambda i: (i, COL_KR // (2 * MLA_ROPE_DIM))),
            pl.BlockSpec((bm, MLA_ROPE_DIM), lambda i: (i, 0)),
            pl.BlockSpec((bm, MLA_ROPE_DIM), lambda i: (i, 0)),
            pl.BlockSpec((MLA_ROPE_DIM, bm), lambda i: (0, i)),
            pl.BlockSpec((MLA_ROPE_DIM, bm), lambda i: (0, i)),
            full((1, MLA_Q_LORA)),
            full((1, MLA_KV_LORA)),
            full((H, 256, MLA_Q_LORA)),
            full((H, MLA_KV_LORA, MLA_NOPE_DIM)),
            full((H, MLA_V_DIM, MLA_KV_LORA)),
        ],
        out_specs=[
            pl.BlockSpec((1, H, MLA_QK_DIM, bm), lambda i: (i // spb, 0, 0, i % spb)),
            pl.BlockSpec((1, H, bm, MLA_QK_DIM), lambda i: (i // spb, 0, i % spb, 0)),
            pl.BlockSpec((1, H, MLA_V_PAD, bm), lambda i: (i // spb, 0, 0, i % spb)),
        ],
        out_shape=[
            jax.ShapeDtypeStruct((batch, H, MLA_QK_DIM, seq), BF16),
            jax.ShapeDtypeStruct((batch, H, seq, MLA_QK_DIM), BF16),
            jax.ShapeDtypeStruct((batch, H, MLA_V_PAD, seq), BF16),
        ],
        compiler_params=_params("parallel"),
        name="mla_proj",
    )(h, h, h, cos2, sin2, cos2_t, sin2_t, qg, kvg, wq_t, wk, wv_t)


def _attn_kernel(q_t_ref, k_ref, v_t_ref, o_ref, s_ref, mx_ref, p_ref, alpha_ref, m_ref, acc_ref, *, tq):
    qi = pl.program_id(2)
    n_rt = tq // ATTN_TILE
    n_full = qi * n_rt
    half = ATTN_TILE // 2
    m_ref[...] = jnp.full(m_ref.shape, MASK_VALUE, F32)
    acc_ref[...] = jnp.zeros(acc_ref.shape, F32)

    def key_start(tile):
        return pl.multiple_of(tile * ATTN_TILE, ATTN_TILE)

    def scores(tile, slot, rt, on_diag):
        q_t = q_t_ref[0, 0, :, rt * ATTN_TILE:(rt + 1) * ATTN_TILE]
        mx = None
        for part in range(2):
            k = k_ref[0, 0, pl.ds(key_start(tile) + part * half, half), :]
            s = jnp.dot(k, q_t, preferred_element_type=F32)
            if on_diag is not False:
                kc = (lax.broadcasted_iota(jnp.int32, s.shape, 0) + part * half) // CHUNK
                qc = lax.broadcasted_iota(jnp.int32, s.shape, 1) // CHUNK
                min_lag = 0 if on_diag is True else jnp.where(on_diag, 0, -ATTN_TILE)
                s = jnp.where(qc - kc >= min_lag, s, MASK_VALUE)
            s_ref[slot, rt, part * half:(part + 1) * half, :] = s
            part_max = jnp.max(s, axis=0, keepdims=True)
            mx = part_max if mx is None else jnp.maximum(mx, part_max)
        mx_ref[slot, rt] = jnp.broadcast_to(mx, mx_ref.shape[2:])

    def softmax(slot, rt):
        m_old = m_ref[rt, 0:1, :]
        m_new = jnp.maximum(m_old, mx_ref[slot, rt, 0:1, :])
        for part in range(2):
            rows = slice(part * half, (part + 1) * half)
            p_ref[slot, rt, rows, :] = jnp.exp2(s_ref[slot, rt, rows, :] - m_new).astype(BF16)
        alpha_ref[slot, rt] = jnp.broadcast_to(jnp.exp2(m_old - m_new), alpha_ref.shape[2:])
        m_ref[rt] = jnp.broadcast_to(m_new, m_ref.shape[1:])

    def accumulate(tile, slot, rt):
        v_t = v_t_ref[0, 0, :, pl.ds(key_start(tile), ATTN_TILE)]
        pv = jnp.dot(v_t, p_ref[slot, rt], preferred_element_type=F32)
        acc_ref[rt] = alpha_ref[slot, rt, 0:1, :] * acc_ref[rt] + pv

    def step(tile, slot, diag, next_is_first_overlap):
        if diag is None:
            cur, nxt = range(n_rt), range(n_rt)
        else:
            cur, nxt = range(diag, n_rt), range(diag + 1, n_rt)
        for rt in nxt:
            if diag is None:
                on_diag = next_is_first_overlap if rt == 0 else False
            else:
                on_diag = rt == diag + 1
            scores(tile + 1, 1 - slot, rt, on_diag)
        for rt in cur:
            softmax(slot, rt)
        for rt in cur:
            accumulate(tile, slot, rt)

    def step_group(tile, diag):
        for i in range(ATTN_GROUP):
            if diag is not None:
                step(tile + i, i % 2, diag + i, False)
            elif i < ATTN_GROUP - 1:
                step(tile + i, i % 2, None, False)
            else:
                step(tile + i, i % 2, None, tile + ATTN_GROUP == n_full)

    for rt in range(n_rt):
        scores(0, 0, rt, (qi == 0) if rt == 0 else False)

    def body(u, carry):
        t = ATTN_GROUP * u
        pl.when(t < n_full)(functools.partial(step_group, t, None))
        for d in range(0, n_rt, ATTN_GROUP):
            pl.when(t == n_full + d)(functools.partial(step_group, t, d))
        return carry

    assert n_rt % ATTN_GROUP == 0 and ATTN_GROUP % 2 == 0
    lax.fori_loop(0, (n_full + n_rt) // ATTN_GROUP, body, 0)
    for rt in range(n_rt):
        acc = acc_ref[rt]
        o_t = acc[:MLA_V_DIM] / acc[MLA_V_DIM:MLA_V_DIM + 1]
        o_ref[rt * ATTN_TILE:(rt + 1) * ATTN_TILE, :] = o_t.T.astype(BF16)


def mla_attention(q_t, k, v_t, tq=1024):
    batch, H, seq, _ = k.shape
    nq = seq // tq
    n_rt = tq // ATTN_TILE
    return pl.pallas_call(
        functools.partial(_attn_kernel, tq=tq),
        grid=(batch, H, nq),
        in_specs=[
            pl.BlockSpec((1, 1, MLA_QK_DIM, tq), lambda b, h, i: (b, h, 0, i)),
            pl.BlockSpec((1, 1, seq, MLA_QK_DIM), lambda b, h, i: (b, h, 0, 0)),
            pl.BlockSpec((1, 1, MLA_V_PAD, seq), lambda b, h, i: (b, h, 0, 0)),
        ],
        out_specs=pl.BlockSpec((tq, MLA_V_DIM), lambda b, h, i: (b * nq + i, h)),
        out_shape=jax.ShapeDtypeStruct((batch * seq, H * MLA_V_DIM), BF16),
        scratch_shapes=[
            pltpu.VMEM((2, n_rt, ATTN_TILE, ATTN_TILE), F32),
            pltpu.VMEM((2, n_rt, 8, ATTN_TILE), F32),
            pltpu.VMEM((2, n_rt, ATTN_TILE, ATTN_TILE), BF16),
            pltpu.VMEM((2, n_rt, 8, ATTN_TILE), F32),
            pltpu.VMEM((n_rt, 8, ATTN_TILE), F32),
            pltpu.VMEM((n_rt, MLA_V_PAD, ATTN_TILE), F32),
        ],
        compiler_params=_params("parallel", "parallel", "arbitrary"),
        name="mla_attention",
    )(q_t, k, v_t)


def _ret_kernel(lg_ref, rq_ref, rk_ref, rv_ref, rg_ref, cos_ref, sin_ref, gng_ref, gnb_ref,
                o_ref, state_ref, decay_ref, qdec_ref, kdec_ref, *, T):
    h = pl.program_id(1)
    c = pl.program_id(2)
    lg = lg_ref[h]
    half = RET_QK_DIM // 2

    @pl.when(c == 0)
    def _():
        state_ref[...] = jnp.zeros_like(state_ref)
        row = lax.broadcasted_iota(jnp.int32, (T, T), 0)
        col = lax.broadcasted_iota(jnp.int32, (T, T), 1)
        dist = jnp.abs(row - col).astype(F32)
        decay_ref[...] = jnp.where(row // CHUNK >= col // CHUNK, jnp.exp(lg * dist), 0.0)
        pos = lax.broadcasted_iota(jnp.int32, (T, 1), 0).astype(F32)
        qdec_ref[...] = jnp.exp(lg * (pos + 1.0))
        kdec_ref[...] = jnp.exp(lg * (T - 1.0 - pos))

    cos = cos_ref[...]
    sin = sin_ref[...]

    def rope(t):
        t1, t2 = t[:, :half], t[:, half:]
        return jnp.concatenate([t1 * cos - t2 * sin, t2 * cos + t1 * sin], axis=-1)

    q = rope(rq_ref[...].astype(F32)) * (RET_QK_DIM ** -0.5)
    k = rope(rk_ref[...].astype(F32))
    v = rv_ref[...]

    qb = q.astype(BF16)
    s = lax.dot_general(qb, k.astype(BF16), (((1,), (1,)), ((), ())), preferred_element_type=F32)
    o = jnp.dot((s * decay_ref[...]).astype(BF16), v, preferred_element_type=F32)
    state = state_ref[...]
    o = o + jnp.dot((q * qdec_ref[...]).astype(BF16), state.astype(BF16), preferred_element_type=F32)
    kv = lax.dot_general((k * kdec_ref[...]).astype(BF16), v, (((0,), (0,)), ((), ())),
                         preferred_element_type=F32)
    state_ref[...] = state * jnp.exp(lg * T) + kv

    mu = jnp.mean(o, axis=-1, keepdims=True)
    d = o - mu
    var = jnp.mean(d * d, axis=-1, keepdims=True)
    o = d * lax.rsqrt(var + GN_EPS) * gng_ref[...] + gnb_ref[...]
    g = rg_ref[...].astype(F32)
    o_ref[...] = (g * jax.nn.sigmoid(g) * o).astype(BF16)


def retention(h, log_gamma, cos_r, sin_r, gn_g, gn_b, batch, seq, T=512):
    n = h.shape[0]
    H = RET_HEADS
    nc = seq // T
    W = RET_QK_DIM

    def col(base):
        return pl.BlockSpec((T, W), lambda b, hh, c: (b * nc + c, base // W + hh))

    tab = pl.BlockSpec((T, W // 2), lambda b, hh, c: (b * nc + c, 0))
    vec = pl.BlockSpec((1, W), lambda b, hh, c: (0, hh))
    return pl.pallas_call(
        functools.partial(_ret_kernel, T=T),
        grid=(batch, H, nc),
        in_specs=[pl.BlockSpec(memory_space=pltpu.SMEM),
                  col(COL_RQ), col(COL_RK), col(COL_RV), col(COL_RG), tab, tab, vec, vec],
        out_specs=pl.BlockSpec((T, W), lambda b, hh, c: (b * nc + c, hh)),
        out_shape=jax.ShapeDtypeStruct((n, H * RET_V_DIM), BF16),
        scratch_shapes=[pltpu.VMEM((RET_QK_DIM, RET_V_DIM), F32), pltpu.VMEM((T, T), F32),
                        pltpu.VMEM((T, 1), F32), pltpu.VMEM((T, 1), F32)],
        compiler_params=_params("parallel", "parallel", "arbitrary"),
        name="retention",
    )(log_gamma, h, h, h, h, cos_r, sin_r, gn_g.reshape(1, -1), gn_b.reshape(1, -1))


def _out_ln_kernel(a_ref, r_ref, wa_ref, wr_ref, x_ref, g_ref, b_ref, o_ref, *, alpha):
    for s in range(a_ref.shape[0] // OUT_SUB):
        rows = slice(s * OUT_SUB, (s + 1) * OUT_SUB)
        mix = jnp.dot(a_ref[rows, :], wa_ref[...], preferred_element_type=F32)
        mix = mix + jnp.dot(r_ref[rows, :], wr_ref[...], preferred_element_type=F32)
        o_ref[rows, :] = _layer_norm(alpha * x_ref[rows, :] + mix, g_ref[...], b_ref[...])


def out_proj_ln(a, r, w, layer, x, g, b, alpha, bm=512):
    n, d = x.shape
    ka, kr = a.shape[1], r.shape[1]
    assert ka == kr
    row = pl.BlockSpec((bm, d), lambda i: (i, 0))
    vec = pl.BlockSpec((1, d), lambda i: (0, 0))
    return pl.pallas_call(
        functools.partial(_out_ln_kernel, alpha=alpha),
        grid=(n // bm,),
        in_specs=[pl.BlockSpec((bm, ka), lambda i: (i, 0)),
                  pl.BlockSpec((bm, kr), lambda i: (i, 0)),
                  pl.BlockSpec((None, ka, d), lambda i: (layer, 0, 0)),
                  pl.BlockSpec((None, kr, d), lambda i: (layer, 1, 0)),
                  row, vec, vec],
        out_specs=row,
        out_shape=jax.ShapeDtypeStruct((n, d), F32),
        compiler_params=_params("parallel"),
        name="out_proj_ln",
    )(a, r, w, w, x, g.reshape(1, d), b.reshape(1, d))


def _ffn_ln_kernel(x_ref, wg_ref, wu_ref, wd_ref, g_ref, b_ref, o_ref, xb_ref, *, alpha):
    j = pl.program_id(1)

    @pl.when(j == 0)
    def _():
        xb_ref[...] = x_ref[...].astype(BF16)
        o_ref[...] = jnp.zeros_like(o_ref)

    xb = xb_ref[...]
    hidden = []
    for s in range(wg_ref.shape[1] // FFN_SUB):
        cols = slice(s * FFN_SUB, (s + 1) * FFN_SUB)
        g = jnp.dot(xb, wg_ref[:, cols], preferred_element_type=F32)
        u = jnp.dot(xb, wu_ref[:, cols], preferred_element_type=F32)
        hidden.append((g * jax.nn.sigmoid(g) * u).astype(BF16))
    hm = jnp.concatenate(hidden, axis=1)
    for c in range(o_ref.shape[1] // FFN_OUT_CHUNK):
        cols = slice(c * FFN_OUT_CHUNK, (c + 1) * FFN_OUT_CHUNK)
        o_ref[:, cols] += jnp.dot(hm, wd_ref[:, cols], preferred_element_type=F32)

    @pl.when(j == pl.num_programs(1) - 1)
    def _():
        o_ref[...] = _layer_norm(alpha * x_ref[...] + o_ref[...], g_ref[...], b_ref[...])


def ffn_ln(x, wg, wu, wd, layer, g, b, alpha, bm=1024, bn=256):
    n, d = x.shape
    dff = wg.shape[2]
    row = pl.BlockSpec((bm, d), lambda i, j: (i, 0))
    vec = pl.BlockSpec((1, d), lambda i, j: (0, 0))
    wcol = pl.BlockSpec((None, d, bn), lambda i, j: (layer, 0, j))
    return pl.pallas_call(
        functools.partial(_ffn_ln_kernel, alpha=alpha),
        grid=(n // bm, dff // bn),
        in_specs=[row, wcol, wcol, pl.BlockSpec((None, bn, d), lambda i, j: (layer, j, 0)), vec, vec],
        out_specs=row,
        out_shape=jax.ShapeDtypeStruct((n, d), F32),
        scratch_shapes=[pltpu.VMEM((bm, d), BF16)],
        compiler_params=_params("parallel", "arbitrary"),
        name="ffn_ln",
    )(x, wg, wu, wd, g.reshape(1, d), b.reshape(1, d))


def _rotate_half_cols(w):
    half = w.shape[-1] // 2
    return jnp.concatenate([-w[..., half:], w[..., :half]], axis=-1)


def _prep_w_in(w):
    w_mla = w[..., :W_IN_MLA_COLS].astype(BF16)
    kr = w_mla[..., COL_KR:]
    pad = jnp.zeros(w.shape[:-1] + (MLA_IN_WIDTH - W_IN_MLA_COLS - MLA_ROPE_DIM,), BF16)
    w_mla = jnp.concatenate([w_mla, _rotate_half_cols(kr), pad], axis=-1)
    return w_mla, w[..., W_IN_MLA_COLS:].astype(BF16)


def _prep_w_uq(w):
    w = w.astype(BF16).reshape(MLA_Q_LORA, MLA_HEADS, MLA_QK_DIM)
    rope = w[..., MLA_NOPE_DIM:]
    w = jnp.concatenate([w, _rotate_half_cols(rope)], axis=-1)
    return w.transpose(1, 2, 0)


def _prep_w_ukv(w):
    w = w.astype(BF16).reshape(MLA_KV_LORA, MLA_HEADS, MLA_NOPE_DIM + MLA_V_DIM)
    return w[..., :MLA_NOPE_DIM].transpose(1, 0, 2), w[..., MLA_NOPE_DIM:].transpose(1, 2, 0)


def _rope_tables(positions, dim):
    inv_freq = ROPE_THETA ** (-jnp.arange(0, dim, 2, dtype=F32) / dim)
    ang = positions.astype(F32)[..., None] * inv_freq
    return jnp.cos(ang), jnp.sin(ang)


def kernel(x, positions, ln_in_g, ln_in_b, w_in, q_norm_g, kv_norm_g, w_uq, w_ukv, ret_gn_g, ret_gn_b, w_out, ln1_g, ln1_b, w_gate, w_up, w_down, ln2_g, ln2_b):
    batch, seq, d = x.shape
    depth = w_in.shape[0]
    n = batch * seq
    alpha = (2 * depth) ** 0.25

    cos_m, sin_m = _rope_tables(positions, MLA_ROPE_DIM)
    cos_r, sin_r = _rope_tables(positions, RET_QK_DIM)
    cos2 = jnp.concatenate([cos_m, cos_m], axis=-1).reshape(n, MLA_ROPE_DIM)
    sin2 = jnp.concatenate([sin_m, sin_m], axis=-1).reshape(n, MLA_ROPE_DIM)
    cos2_t, sin2_t = cos2.T, sin2.T
    cos_r = cos_r.reshape(n, RET_QK_DIM // 2)
    sin_r = sin_r.reshape(n, RET_QK_DIM // 2)
    log_gamma = jnp.log1p(-jnp.exp2(-5.0 - jnp.arange(RET_HEADS, dtype=F32)))

    w_in_mla, w_in_ret = _prep_w_in(w_in)
    w_out_b = w_out.astype(BF16)
    w_gate_b, w_up_b, w_down_b = w_gate.astype(BF16), w_up.astype(BF16), w_down.astype(BF16)

    xf = ln_in(x.reshape(n, d), ln_in_g, ln_in_b)
    for l in range(depth):
        h_mla, h_ret = in_proj(xf, w_in_mla, w_in_ret, l)
        wk, wv_t = _prep_w_ukv(w_ukv[l])
        q_t, k, v_t = mla_proj(h_mla, cos2, sin2, cos2_t, sin2_t, q_norm_g[l].reshape(1, -1),
                               kv_norm_g[l].reshape(1, -1), _prep_w_uq(w_uq[l]), wk, wv_t, batch, seq)
        a = mla_attention(q_t, k, v_t)
        r = retention(h_ret, log_gamma, cos_r, sin_r, ret_gn_g[l], ret_gn_b[l], batch, seq)
        xf = out_proj_ln(a, r, w_out_b, l, xf, ln1_g[l], ln1_b[l], alpha)
        xf = ffn_ln(xf, w_gate_b, w_up_b, w_down_b, l, ln2_g[l], ln2_b[l], alpha)
    return xf.reshape(batch, seq, d)
```

```python
import functools

import jax
import jax.numpy as jnp
from jax import lax
from jax.experimental import pallas as pl
from jax.experimental.pallas import tpu as pltpu

F32 = jnp.float32
BF16 = jnp.bfloat16

D_MODEL = 2048
CHUNK = 64
MLA_HEADS = 8
MLA_Q_LORA = 512
MLA_KV_LORA = 256
MLA_NOPE_DIM = 128
MLA_ROPE_DIM = 64
MLA_V_DIM = 128
MLA_QK_DIM = MLA_NOPE_DIM + MLA_ROPE_DIM
RET_HEADS = 4
RET_QK_DIM = 256
RET_V_DIM = 256
ROPE_THETA = 10000.0
LN_EPS = 1e-5
RMS_EPS = 1e-6
GN_EPS = 1e-5

COL_CQ = 0
COL_CKV = 512
COL_KR = 768
MLA_IN_WIDTH = 1024
COL_RQ = 0
COL_RK = 1024
COL_RV = 2048
COL_RG = 3072
W_IN_MLA_COLS = 832

VMEM_LIMIT = 60 * 1024 * 1024

MASK_VALUE = -1e30
LOG2E = 1.4426950408889634
FFN_SUB = 256
FFN_OUT_CHUNK = 512
FFN_ROWS = 256
OUT_SUB = 256
ATTN_TILE = 256
ATTN_GROUP = 4
MLA_V_PAD = MLA_V_DIM + 16


def _params(*sem):
    return pltpu.CompilerParams(dimension_semantics=sem, vmem_limit_bytes=VMEM_LIMIT)


def _layer_norm(y, g, b):
    mu = jnp.mean(y, axis=-1, keepdims=True)
    d = y - mu
    var = jnp.mean(d * d, axis=-1, keepdims=True)
    return d * lax.rsqrt(var + LN_EPS) * g + b


def _ln_in_kernel(x_ref, g_ref, b_ref, o_ref):
    o_ref[...] = _layer_norm(x_ref[...], g_ref[...], b_ref[...])


def ln_in(x, g, b, bm=512):
    n, d = x.shape
    row = pl.BlockSpec((bm, d), lambda i: (i, 0))
    vec = pl.BlockSpec((1, d), lambda i: (0, 0))
    return pl.pallas_call(
        _ln_in_kernel,
        grid=(n // bm,),
        in_specs=[row, vec, vec],
        out_specs=row,
        out_shape=jax.ShapeDtypeStruct((n, d), F32),
        compiler_params=_params("parallel"),
        name="ln_in",
    )(x, g.reshape(1, d), b.reshape(1, d))


def _in_proj_kernel(x_ref, wm_ref, wr_ref, om_ref, or_ref, xb_ref):
    j = pl.program_id(1)

    @pl.when(j == 0)
    def _():
        xb_ref[...] = x_ref[...].astype(BF16)
        om_ref[...] = jnp.dot(xb_ref[...], wm_ref[...], preferred_element_type=F32).astype(BF16)

    @pl.when(j > 0)
    def _():
        or_ref[...] = jnp.dot(xb_ref[...], wr_ref[...], preferred_element_type=F32).astype(BF16)


def in_proj(x, w_mla, w_ret, layer, bm=1024):
    m, k = x.shape
    bn = w_mla.shape[2]
    n_ret = w_ret.shape[2]
    return pl.pallas_call(
        _in_proj_kernel,
        grid=(m // bm, 1 + n_ret // bn),
        in_specs=[pl.BlockSpec((bm, k), lambda i, j: (i, 0)),
                  pl.BlockSpec((None, k, bn), lambda i, j: (layer, 0, 0)),
                  pl.BlockSpec((None, k, bn), lambda i, j: (layer, 0, jnp.maximum(j - 1, 0)))],
        out_specs=[pl.BlockSpec((bm, bn), lambda i, j: (i, 0)),
                   pl.BlockSpec((bm, bn), lambda i, j: (i, jnp.maximum(j - 1, 0)))],
        out_shape=[jax.ShapeDtypeStruct((m, bn), BF16), jax.ShapeDtypeStruct((m, n_ret), BF16)],
        scratch_shapes=[pltpu.VMEM((bm, k), BF16)],
        compiler_params=_params("parallel", "arbitrary"),
        name="in_proj",
    )(x, w_mla, w_ret)


def _rms_norm(t, g):
    return t * lax.rsqrt(jnp.mean(t * t, axis=-1, keepdims=True) + RMS_EPS) * g


def _mla_proj_kernel(cq_ref, ckv_ref, kr_ref, cos_ref, sin_ref, cos_t_ref, sin_t_ref, qg_ref, kvg_ref,
                     wq_t_ref, wk_ref, wv_t_ref, q_t_ref, k_ref, v_t_ref, *, q_scale):
    cqn = _rms_norm(cq_ref[...].astype(F32), qg_ref[...])
    ckvn = _rms_norm(ckv_ref[...].astype(F32), kvg_ref[...])
    cqn_t = cqn.T.astype(BF16)
    ckvn_t = ckvn.T.astype(BF16)
    ckvn = ckvn.astype(BF16)
    kr = kr_ref[...].astype(F32)
    k_rope = (kr[:, :MLA_ROPE_DIM] * cos_ref[...] + kr[:, MLA_ROPE_DIM:] * sin_ref[...]).astype(BF16)
    cos_t = cos_t_ref[...]
    sin_t = sin_t_ref[...]
    ones = jnp.ones((MLA_V_PAD - MLA_V_DIM, cqn_t.shape[1]), BF16)
    for h in range(MLA_HEADS):
        qh = jnp.dot(wq_t_ref[h], cqn_t, preferred_element_type=F32)
        q_t_ref[0, h, :MLA_NOPE_DIM, :] = (qh[:MLA_NOPE_DIM] * q_scale).astype(BF16)
        q_rope = qh[MLA_NOPE_DIM:MLA_QK_DIM] * cos_t + qh[MLA_QK_DIM:] * sin_t
        q_t_ref[0, h, MLA_NOPE_DIM:, :] = (q_rope * q_scale).astype(BF16)
        k_ref[0, h, :, :MLA_NOPE_DIM] = jnp.dot(ckvn, wk_ref[h], preferred_element_type=F32).astype(BF16)
        k_ref[0, h, :, MLA_NOPE_DIM:] = k_rope
        v_t_ref[0, h, :MLA_V_DIM, :] = jnp.dot(wv_t_ref[h], ckvn_t, preferred_element_type=F32).astype(BF16)
        v_t_ref[0, h, MLA_V_DIM:, :] = ones


def mla_proj(h, cos2, sin2, cos2_t, sin2_t, qg, kvg, wq_t, wk, wv_t, batch, seq, bm=512):
    n = h.shape[0]
    spb = seq // bm
    H = MLA_HEADS
    q_scale = (MLA_QK_DIM ** -0.5) * LOG2E
    full = lambda shape: pl.BlockSpec(shape, lambda i: (0,) * len(shape))
    return pl.pallas_call(
        functools.partial(_mla_proj_kernel, q_scale=q_scale),
        grid=(n // bm,),
        in_specs=[
            pl.BlockSpec((bm, MLA_Q_LORA), lambda i: (i, COL_CQ // MLA_Q_LORA)),
            pl.BlockSpec((bm, MLA_KV_LORA), lambda i: (i, COL_CKV // MLA_KV_LORA)),
            pl.BlockSpec((bm, 2 * MLA_ROPE_DIM), lambda i: (i, COL_KR // (2 * MLA_ROPE_DIM))),
            pl.BlockSpec((bm, MLA_ROPE_DIM), lambda i: (i, 0)),
            pl.BlockSpec((bm, MLA_ROPE_DIM), lambda i: (i, 0)),
            pl.BlockSpec((MLA_ROPE_DIM, bm), lambda i: (0, i)),
            pl.BlockSpec((MLA_ROPE_DIM, bm), lambda i: (0, i)),
            full((1, MLA_Q_LORA)),
            full((1, MLA_KV_LORA)),
            full((H, 256, MLA_Q_LORA)),
            full((H, MLA_KV_LORA, MLA_NOPE_DIM)),
            full((H, MLA_V_DIM, MLA_KV_LORA)),
        ],
        out_specs=[
            pl.BlockSpec((1, H, MLA_QK_DIM, bm), lambda i: (i // spb, 0, 0, i % spb)),
            pl.BlockSpec((1, H, bm, MLA_QK_DIM), lambda i: (i // spb, 0, i % spb, 0)),
            pl.BlockSpec((1, H, MLA_V_PAD, bm), lambda i: (i // spb, 0, 0, i % spb)),
        ],
        out_shape=[
            jax.ShapeDtypeStruct((batch, H, MLA_QK_DIM, seq), BF16),
            jax.ShapeDtypeStruct((batch, H, seq, MLA_QK_DIM), BF16),
            jax.ShapeDtypeStruct((batch, H, MLA_V_PAD, seq), BF16),
        ],
        compiler_params=_params("parallel"),
        name="mla_proj",
    )(h, h, h, cos2, sin2, cos2_t, sin2_t, qg, kvg, wq_t, wk, wv_t)


def _attn_kernel(q_t_ref, k_ref, v_t_ref, o_ref, s_ref, mx_ref, p_ref, alpha_ref, m_ref, acc_ref, *, tq):
    qi = pl.program_id(2)
    n_rt = tq // ATTN_TILE
    n_full = qi * n_rt
    half = ATTN_TILE // 2
    m_ref[...] = jnp.full(m_ref.shape, MASK_VALUE, F32)
    acc_ref[...] = jnp.zeros(acc_ref.shape, F32)

    def key_start(tile):
        return pl.multiple_of(tile * ATTN_TILE, ATTN_TILE)

    def scores(tile, slot, rt, on_diag):
        q_t = q_t_ref[0, 0, :, rt * ATTN_TILE:(rt + 1) * ATTN_TILE]
        mx = None
        for part in range(2):
            k = k_ref[0, 0, pl.ds(key_start(tile) + part * half, half), :]
            s = jnp.dot(k, q_t, preferred_element_type=F32)
            if on_diag is not False:
                kc = (lax.broadcasted_iota(jnp.int32, s.shape, 0) + part * half) // CHUNK
                qc = lax.broadcasted_iota(jnp.int32, s.shape, 1) // CHUNK
                min_lag = 0 if on_diag is True else jnp.where(on_diag, 0, -ATTN_TILE)
                s = jnp.where(qc - kc >= min_lag, s, MASK_VALUE)
            s_ref[slot, rt, part * half:(part + 1) * half, :] = s
            part_max = jnp.max(s, axis=0, keepdims=True)
            mx = part_max if mx is None else jnp.maximum(mx, part_max)
        mx_ref[slot, rt] = jnp.broadcast_to(mx, mx_ref.shape[2:])

    def softmax(slot, rt):
        m_old = m_ref[rt, 0:1, :]
        m_new = jnp.maximum(m_old, mx_ref[slot, rt, 0:1, :])
        for part in range(2):
            rows = slice(part * half, (part + 1) * half)
            p_ref[slot, rt, rows, :] = jnp.exp2(s_ref[slot, rt, rows, :] - m_new).astype(BF16)
        alpha_ref[slot, rt] = jnp.broadcast_to(jnp.exp2(m_old - m_new), alpha_ref.shape[2:])
        m_ref[rt] = jnp.broadcast_to(m_new, m_ref.shape[1:])

    def accumulate(tile, slot, rt):
        v_t = v_t_ref[0, 0, :, pl.ds(key_start(tile), ATTN_TILE)]
        pv = jnp.dot(v_t, p_ref[slot, rt], preferred_element_type=F32)
        acc_ref[rt] = alpha_ref[slot, rt, 0:1, :] * acc_ref[rt] + pv

    def step(tile, slot, diag, next_is_first_overlap):
        if diag is None:
            cur, nxt = range(n_rt), range(n_rt)
        else:
            cur, nxt = range(diag, n_rt), range(diag + 1, n_rt)
        for rt in nxt:
            if diag is None:
                on_diag = next_is_first_overlap if rt == 0 else False
            else:
                on_diag = rt == diag + 1
            scores(tile + 1, 1 - slot, rt, on_diag)
        for rt in cur:
            softmax(slot, rt)
        for rt in cur:
            accumulate(tile, slot, rt)

    def step_group(tile, diag):
        for i in range(ATTN_GROUP):
            if diag is not None:
                step(tile + i, i % 2, diag + i, False)
            elif i < ATTN_GROUP - 1:
                step(tile + i, i % 2, None, False)
            else:
                step(tile + i, i % 2, None, tile + ATTN_GROUP == n_full)

    for rt in range(n_rt):
        scores(0, 0, rt, (qi == 0) if rt == 0 else False)

    def body(u, carry):
        t = ATTN_GROUP * u
        pl.when(t < n_full)(functools.partial(step_group, t, None))
        for d in range(0, n_rt, ATTN_GROUP):
            pl.when(t == n_full + d)(functools.partial(step_group, t, d))
        return carry

    assert n_rt % ATTN_GROUP == 0 and ATTN_GROUP % 2 == 0
    lax.fori_loop(0, (n_full + n_rt) // ATTN_GROUP, body, 0)
    for rt in range(n_rt):
        acc = acc_ref[rt]
        o_t = acc[:MLA_V_DIM] / acc[MLA_V_DIM:MLA_V_DIM + 1]
        o_ref[rt * ATTN_TILE:(rt + 1) * ATTN_TILE, :] = o_t.T.astype(BF16)


def mla_attention(q_t, k, v_t, tq=1024):
    batch, H, seq, _ = k.shape
    nq = seq // tq
    n_rt = tq // ATTN_TILE
    return pl.pallas_call(
        functools.partial(_attn_kernel, tq=tq),
        grid=(batch, H, nq),
        in_specs=[
            pl.BlockSpec((1, 1, MLA_QK_DIM, tq), lambda b, h, i: (b, h, 0, i)),
            pl.BlockSpec((1, 1, seq, MLA_QK_DIM), lambda b, h, i: (b, h, 0, 0)),
            pl.BlockSpec((1, 1, MLA_V_PAD, seq), lambda b, h, i: (b, h, 0, 0)),
        ],
        out_specs=pl.BlockSpec((tq, MLA_V_DIM), lambda b, h, i: (b * nq + i, h)),
        out_shape=jax.ShapeDtypeStruct((batch * seq, H * MLA_V_DIM), BF16),
        scratch_shapes=[
            pltpu.VMEM((2, n_rt, ATTN_TILE, ATTN_TILE), F32),
            pltpu.VMEM((2, n_rt, 8, ATTN_TILE), F32),
            pltpu.VMEM((2, n_rt, ATTN_TILE, ATTN_TILE), BF16),
            pltpu.VMEM((2, n_rt, 8, ATTN_TILE), F32),
            pltpu.VMEM((n_rt, 8, ATTN_TILE), F32),
            pltpu.VMEM((n_rt, MLA_V_PAD, ATTN_TILE), F32),
        ],
        compiler_params=_params("parallel", "parallel", "arbitrary"),
        name="mla_attention",
    )(q_t, k, v_t)


def _ret_kernel(lg_ref, rq_ref, rk_ref, rv_ref, rg_ref, cos_ref, sin_ref, gng_ref, gnb_ref,
                o_ref, state_ref, decay_ref, qdec_ref, kdec_ref, *, T):
    h = pl.program_id(1)
    c = pl.program_id(2)
    lg = lg_ref[h]
    half = RET_QK_DIM // 2

    @pl.when(c == 0)
    def _():
        state_ref[...] = jnp.zeros_like(state_ref)
        row = lax.broadcasted_iota(jnp.int32, (T, T), 0)
        col = lax.broadcasted_iota(jnp.int32, (T, T), 1)
        dist = jnp.abs(row - col).astype(F32)
        decay_ref[...] = jnp.where(row // CHUNK >= col // CHUNK, jnp.exp(lg * dist), 0.0)
        pos = lax.broadcasted_iota(jnp.int32, (T, 1), 0).astype(F32)
        qdec_ref[...] = jnp.exp(lg * (pos + 1.0))
        kdec_ref[...] = jnp.exp(lg * (T - 1.0 - pos))

    cos = cos_ref[...]
    sin = sin_ref[...]

    def rope(t):
        t1, t2 = t[:, :half], t[:, half:]
        return jnp.concatenate([t1 * cos - t2 * sin, t2 * cos + t1 * sin], axis=-1)

    q = rope(rq_ref[...].astype(F32)) * (RET_QK_DIM ** -0.5)
    k = rope(rk_ref[...].astype(F32))
    v = rv_ref[...]

    qb = q.astype(BF16)
    s = lax.dot_general(qb, k.astype(BF16), (((1,), (1,)), ((), ())), preferred_element_type=F32)
    o = jnp.dot((s * decay_ref[...]).astype(BF16), v, preferred_element_type=F32)
    state = state_ref[...]
    o = o + jnp.dot((q * qdec_ref[...]).astype(BF16), state.astype(BF16), preferred_element_type=F32)
    kv = lax.dot_general((k * kdec_ref[...]).astype(BF16), v, (((0,), (0,)), ((), ())),
                         preferred_element_type=F32)
    state_ref[...] = state * jnp.exp(lg * T) + kv

    mu = jnp.mean(o, axis=-1, keepdims=True)
    d = o - mu
    var = jnp.mean(d * d, axis=-1, keepdims=True)
    o = d * lax.rsqrt(var + GN_EPS) * gng_ref[...] + gnb_ref[...]
    g = rg_ref[...].astype(F32)
    o_ref[...] = (g * jax.nn.sigmoid(g) * o).astype(BF16)


def retention(h, log_gamma, cos_r, sin_r, gn_g, gn_b, batch, seq, T=512):
    n = h.shape[0]
    H = RET_HEADS
    nc = seq // T
    W = RET_QK_DIM

    def col(base):
        return pl.BlockSpec((T, W), lambda b, hh, c: (b * nc + c, base // W + hh))

    tab = pl.BlockSpec((T, W // 2), lambda b, hh, c: (b * nc + c, 0))
    vec = pl.BlockSpec((1, W), lambda b, hh, c: (0, hh))
    return pl.pallas_call(
        functools.partial(_ret_kernel, T=T),
        grid=(batch, H, nc),
        in_specs=[pl.BlockSpec(memory_space=pltpu.SMEM),
                  col(COL_RQ), col(COL_RK), col(COL_RV), col(COL_RG), tab, tab, vec, vec],
        out_specs=pl.BlockSpec((T, W), lambda b, hh, c: (b * nc + c, hh)),
        out_shape=jax.ShapeDtypeStruct((n, H * RET_V_DIM), BF16),
        scratch_shapes=[pltpu.VMEM((RET_QK_DIM, RET_V_DIM), F32), pltpu.VMEM((T, T), F32),
                        pltpu.VMEM((T, 1), F32), pltpu.VMEM((T, 1), F32)],
        compiler_params=_params("parallel", "parallel", "arbitrary"),
        name="retention",
    )(log_gamma, h, h, h, h, cos_r, sin_r, gn_g.reshape(1, -1), gn_b.reshape(1, -1))


def _out_ln_kernel(a_ref, r_ref, wa_ref, wr_ref, x_ref, g_ref, b_ref, o_ref, *, alpha):
    for s in range(a_ref.shape[0] // OUT_SUB):
        rows = slice(s * OUT_SUB, (s + 1) * OUT_SUB)
        mix = jnp.dot(a_ref[rows, :], wa_ref[...], preferred_element_type=F32)
        mix = mix + jnp.dot(r_ref[rows, :], wr_ref[...], preferred_element_type=F32)
        o_ref[rows, :] = _layer_norm(alpha * x_ref[rows, :] + mix, g_ref[...], b_ref[...])


def out_proj_ln(a, r, w, layer, x, g, b, alpha, bm=1024):
    n, d = x.shape
    ka, kr = a.shape[1], r.shape[1]
    assert ka == kr
    row = pl.BlockSpec((bm, d), lambda i: (i, 0))
    vec = pl.BlockSpec((1, d), lambda i: (0, 0))
    once = pl.Buffered(1)
    return pl.pallas_call(
        functools.partial(_out_ln_kernel, alpha=alpha),
        grid=(n // bm,),
        in_specs=[pl.BlockSpec((bm, ka), lambda i: (i, 0)),
                  pl.BlockSpec((bm, kr), lambda i: (i, 0)),
                  pl.BlockSpec((None, ka, d), lambda i: (layer, 0, 0), pipeline_mode=once),
                  pl.BlockSpec((None, kr, d), lambda i: (layer, 1, 0), pipeline_mode=once),
                  row, vec, vec],
        out_specs=row,
        out_shape=jax.ShapeDtypeStruct((n, d), F32),
        compiler_params=_params("parallel"),
        name="out_proj_ln",
    )(a, r, w, w, x, g.reshape(1, d), b.reshape(1, d))


def _ffn_ln_kernel(x_ref, wg_ref, wu_ref, wd_ref, g_ref, b_ref, o_ref, xb_ref, *, alpha):
    j = pl.program_id(1)
    last = pl.num_programs(1) - 1
    row_tiles = [slice(r, r + FFN_ROWS) for r in range(0, o_ref.shape[0], FFN_ROWS)]

    def hidden(xb):
        parts = []
        for s in range(wg_ref.shape[1] // FFN_SUB):
            cols = slice(s * FFN_SUB, (s + 1) * FFN_SUB)
            g = jnp.dot(xb, wg_ref[:, cols], preferred_element_type=F32)
            u = jnp.dot(xb, wu_ref[:, cols], preferred_element_type=F32)
            parts.append((g * jax.nn.sigmoid(g) * u).astype(BF16))
        return parts[0] if len(parts) == 1 else jnp.concatenate(parts, axis=1)

    @pl.when(j == 0)
    def _():
        for rows in row_tiles:
            xb = x_ref[rows, :].astype(BF16)
            xb_ref[rows, :] = xb
            o_ref[rows, :] = jnp.dot(hidden(xb), wd_ref[...], preferred_element_type=F32)

    @pl.when(jnp.logical_and(j > 0, j < last))
    def _():
        hm = hidden(xb_ref[...])
        for c in range(o_ref.shape[1] // FFN_OUT_CHUNK):
            cols = slice(c * FFN_OUT_CHUNK, (c + 1) * FFN_OUT_CHUNK)
            o_ref[:, cols] += jnp.dot(hm, wd_ref[:, cols], preferred_element_type=F32)

    @pl.when(j == last)
    def _():
        for rows in row_tiles:
            f = o_ref[rows, :] + jnp.dot(hidden(xb_ref[rows, :]), wd_ref[...], preferred_element_type=F32)
            o_ref[rows, :] = _layer_norm(alpha * x_ref[rows, :] + f, g_ref[...], b_ref[...])


def ffn_ln(x, wg, wu, wd, layer, g, b, alpha, bm=1024, bn=512):
    n, d = x.shape
    dff = wg.shape[2]
    row = pl.BlockSpec((bm, d), lambda i, j: (i, 0))
    vec = pl.BlockSpec((1, d), lambda i, j: (0, 0))
    wcol = pl.BlockSpec((None, d, bn), lambda i, j: (layer, 0, j))
    return pl.pallas_call(
        functools.partial(_ffn_ln_kernel, alpha=alpha),
        grid=(n // bm, dff // bn),
        in_specs=[row, wcol, wcol, pl.BlockSpec((None, bn, d), lambda i, j: (layer, j, 0)), vec, vec],
        out_specs=row,
        out_shape=jax.ShapeDtypeStruct((n, d), F32),
        scratch_shapes=[pltpu.VMEM((bm, d), BF16)],
        compiler_params=_params("parallel", "arbitrary"),
        name="ffn_ln",
    )(x, wg, wu, wd, g.reshape(1, d), b.reshape(1, d))


def _rotate_half_cols(w):
    half = w.shape[-1] // 2
    return jnp.concatenate([-w[..., half:], w[..., :half]], axis=-1)


def _prep_w_in(w):
    w_mla = w[..., :W_IN_MLA_COLS].astype(BF16)
    kr = w_mla[..., COL_KR:]
    pad = jnp.zeros(w.shape[:-1] + (MLA_IN_WIDTH - W_IN_MLA_COLS - MLA_ROPE_DIM,), BF16)
    w_mla = jnp.concatenate([w_mla, _rotate_half_cols(kr), pad], axis=-1)
    return w_mla, w[..., W_IN_MLA_COLS:].astype(BF16)


def _prep_w_uq(w):
    w = w.astype(BF16).reshape(MLA_Q_LORA, MLA_HEADS, MLA_QK_DIM)
    rope = w[..., MLA_NOPE_DIM:]
    w = jnp.concatenate([w, _rotate_half_cols(rope)], axis=-1)
    return w.transpose(1, 2, 0)


def _prep_w_ukv(w):
    w = w.astype(BF16).reshape(MLA_KV_LORA, MLA_HEADS, MLA_NOPE_DIM + MLA_V_DIM)
    return w[..., :MLA_NOPE_DIM].transpose(1, 0, 2), w[..., MLA_NOPE_DIM:].transpose(1, 2, 0)


def _rope_tables(positions, dim):
    inv_freq = ROPE_THETA ** (-jnp.arange(0, dim, 2, dtype=F32) / dim)
    ang = positions.astype(F32)[..., None] * inv_freq
    return jnp.cos(ang), jnp.sin(ang)


def kernel(x, positions, ln_in_g, ln_in_b, w_in, q_norm_g, kv_norm_g, w_uq, w_ukv, ret_gn_g, ret_gn_b, w_out, ln1_g, ln1_b, w_gate, w_up, w_down, ln2_g, ln2_b):
    batch, seq, d = x.shape
    depth = w_in.shape[0]
    n = batch * seq
    alpha = (2 * depth) ** 0.25

    cos_m, sin_m = _rope_tables(positions, MLA_ROPE_DIM)
    cos_r, sin_r = _rope_tables(positions, RET_QK_DIM)
    cos2 = jnp.concatenate([cos_m, cos_m], axis=-1).reshape(n, MLA_ROPE_DIM)
    sin2 = jnp.concatenate([sin_m, sin_m], axis=-1).reshape(n, MLA_ROPE_DIM)
    cos2_t, sin2_t = cos2.T, sin2.T
    cos_r = cos_r.reshape(n, RET_QK_DIM // 2)
    sin_r = sin_r.reshape(n, RET_QK_DIM // 2)
    log_gamma = jnp.log1p(-jnp.exp2(-5.0 - jnp.arange(RET_HEADS, dtype=F32)))

    w_in_mla, w_in_ret = _prep_w_in(w_in)
    w_out_b = w_out.astype(BF16)
    w_gate_b, w_up_b, w_down_b = w_gate.astype(BF16), w_up.astype(BF16), w_down.astype(BF16)

    xf = ln_in(x.reshape(n, d), ln_in_g, ln_in_b)
    for l in range(depth):
        h_mla, h_ret = in_proj(xf, w_in_mla, w_in_ret, l)
        wk, wv_t = _prep_w_ukv(w_ukv[l])
        q_t, k, v_t = mla_proj(h_mla, cos2, sin2, cos2_t, sin2_t, q_norm_g[l].reshape(1, -1),
                               kv_norm_g[l].reshape(1, -1), _prep_w_uq(w_uq[l]), wk, wv_t, batch, seq)
        a = mla_attention(q_t, k, v_t)
        r = retention(h_ret, log_gamma, cos_r, sin_r, ret_gn_g[l], ret_gn_b[l], batch, seq)
        xf = out_proj_ln(a, r, w_out_b, l, xf, ln1_g[l], ln1_b[l], alpha)
        xf = ffn_ln(xf, w_gate_b, w_up_b, w_down_b, l, ln2_g[l], ln2_b[l], alpha)
    return xf.reshape(batch, seq, d)
```

```python
import functools

import jax
import jax.numpy as jnp
from jax import lax
from jax.experimental import pallas as pl
from jax.experimental.pallas import tpu as pltpu

F32 = jnp.float32
BF16 = jnp.bfloat16

D_MODEL = 2048
CHUNK = 64
MLA_HEADS = 8
MLA_Q_LORA = 512
MLA_KV_LORA = 256
MLA_NOPE_DIM = 128
MLA_ROPE_DIM = 64
MLA_V_DIM = 128
MLA_QK_DIM = MLA_NOPE_DIM + MLA_ROPE_DIM
RET_HEADS = 4
RET_QK_DIM = 256
RET_V_DIM = 256
ROPE_THETA = 10000.0
LN_EPS = 1e-5
RMS_EPS = 1e-6
GN_EPS = 1e-5

COL_CQ = 0
COL_CKV = 512
COL_KR = 768
MLA_IN_WIDTH = 1024
COL_RQ = 0
COL_RK = 1024
COL_RV = 2048
COL_RG = 3072
W_IN_MLA_COLS = 832

VMEM_LIMIT = 60 * 1024 * 1024

MASK_VALUE = -1e30
LOG2E = 1.4426950408889634
FFN_SUB = 256
FFN_OUT_CHUNK = 512
FFN_ROWS = 256
OUT_SUB = 256
ATTN_TILE = 256
ATTN_GROUP = 4
MLA_V_PAD = MLA_V_DIM + 16


def _params(*sem):
    return pltpu.CompilerParams(dimension_semantics=sem, vmem_limit_bytes=VMEM_LIMIT)


def _layer_norm(y, g, b):
    mu = jnp.mean(y, axis=-1, keepdims=True)
    d = y - mu
    var = jnp.mean(d * d, axis=-1, keepdims=True)
    return d * lax.rsqrt(var + LN_EPS) * g + b


def _ln_in_kernel(x_ref, g_ref, b_ref, o_ref):
    o_ref[...] = _layer_norm(x_ref[...], g_ref[...], b_ref[...])


def ln_in(x, g, b, bm=512):
    n, d = x.shape
    row = pl.BlockSpec((bm, d), lambda i: (i, 0))
    vec = pl.BlockSpec((1, d), lambda i: (0, 0))
    return pl.pallas_call(
        _ln_in_kernel,
        grid=(n // bm,),
        in_specs=[row, vec, vec],
        out_specs=row,
        out_shape=jax.ShapeDtypeStruct((n, d), F32),
        compiler_params=_params("parallel"),
        name="ln_in",
    )(x, g.reshape(1, d), b.reshape(1, d))


def _in_proj_kernel(x_ref, wm_ref, wr_ref, om_ref, or_ref, xb_ref):
    j = pl.program_id(1)

    @pl.when(j == 0)
    def _():
        xb_ref[...] = x_ref[...].astype(BF16)
        om_ref[...] = jnp.dot(xb_ref[...], wm_ref[...], preferred_element_type=F32).astype(BF16)

    @pl.when(j > 0)
    def _():
        or_ref[...] = jnp.dot(xb_ref[...], wr_ref[...], preferred_element_type=F32).astype(BF16)


def in_proj(x, w_mla, w_ret, layer, bm=1024):
    m, k = x.shape
    bn = w_mla.shape[2]
    n_ret = w_ret.shape[2]
    return pl.pallas_call(
        _in_proj_kernel,
        grid=(m // bm, 1 + n_ret // bn),
        in_specs=[pl.BlockSpec((bm, k), lambda i, j: (i, 0)),
                  pl.BlockSpec((None, k, bn), lambda i, j: (layer, 0, 0)),
                  pl.BlockSpec((None, k, bn), lambda i, j: (layer, 0, jnp.maximum(j - 1, 0)))],
        out_specs=[pl.BlockSpec((bm, bn), lambda i, j: (i, 0)),
                   pl.BlockSpec((bm, bn), lambda i, j: (i, jnp.maximum(j - 1, 0)))],
        out_shape=[jax.ShapeDtypeStruct((m, bn), BF16), jax.ShapeDtypeStruct((m, n_ret), BF16)],
        scratch_shapes=[pltpu.VMEM((bm, k), BF16)],
        compiler_params=_params("parallel", "arbitrary"),
        name="in_proj",
    )(x, w_mla, w_ret)


def _rms_norm(t, g):
    return t * lax.rsqrt(jnp.mean(t * t, axis=-1, keepdims=True) + RMS_EPS) * g


def _mla_proj_kernel(cq_ref, ckv_ref, kr_ref, cos_ref, sin_ref, cos_t_ref, sin_t_ref, qg_ref, kvg_ref,
                     wq_t_ref, wk_ref, wv_t_ref, q_t_ref, k_ref, v_t_ref, *, q_scale):
    cqn = _rms_norm(cq_ref[...].astype(F32), qg_ref[...])
    ckvn = _rms_norm(ckv_ref[...].astype(F32), kvg_ref[...])
    cqn_t = cqn.T.astype(BF16)
    ckvn_t = ckvn.T.astype(BF16)
    ckvn = ckvn.astype(BF16)
    kr = kr_ref[...].astype(F32)
    k_rope = (kr[:, :MLA_ROPE_DIM] * cos_ref[...] + kr[:, MLA_ROPE_DIM:] * sin_ref[...]).astype(BF16)
    cos_t = cos_t_ref[...]
    sin_t = sin_t_ref[...]
    ones = jnp.ones((MLA_V_PAD - MLA_V_DIM, cqn_t.shape[1]), BF16)
    k_nope = jnp.dot(ckvn, wk_ref[...], preferred_element_type=F32).astype(BF16)
    rope_mid = MLA_NOPE_DIM + MLA_ROPE_DIM // 2
    for h in range(MLA_HEADS):
        qh = jnp.dot(wq_t_ref[h], cqn_t, preferred_element_type=F32)
        q_t_ref[0, h, :MLA_NOPE_DIM, :] = (qh[:MLA_NOPE_DIM] * q_scale).astype(BF16)
        r1, r2 = qh[MLA_NOPE_DIM:rope_mid], qh[rope_mid:]
        q_t_ref[0, h, MLA_NOPE_DIM:rope_mid, :] = ((r1 * cos_t - r2 * sin_t) * q_scale).astype(BF16)
        q_t_ref[0, h, rope_mid:, :] = ((r2 * cos_t + r1 * sin_t) * q_scale).astype(BF16)
        k_ref[0, h, :, :MLA_NOPE_DIM] = k_nope[:, h * MLA_NOPE_DIM:(h + 1) * MLA_NOPE_DIM]
        k_ref[0, h, :, MLA_NOPE_DIM:] = k_rope
        v_t_ref[0, h, :MLA_V_DIM, :] = jnp.dot(wv_t_ref[h], ckvn_t, preferred_element_type=F32).astype(BF16)
        v_t_ref[0, h, MLA_V_DIM:, :] = ones


def mla_proj(h, cos2, sin2, cos_t, sin_t, qg, kvg, wq_t, wk, wv_t, batch, seq, bm=512):
    n = h.shape[0]
    spb = seq // bm
    H = MLA_HEADS
    q_scale = (MLA_QK_DIM ** -0.5) * LOG2E
    full = lambda shape: pl.BlockSpec(shape, lambda i: (0,) * len(shape))
    return pl.pallas_call(
        functools.partial(_mla_proj_kernel, q_scale=q_scale),
        grid=(n // bm,),
        in_specs=[
            pl.BlockSpec((bm, MLA_Q_LORA), lambda i: (i, COL_CQ // MLA_Q_LORA)),
            pl.BlockSpec((bm, MLA_KV_LORA), lambda i: (i, COL_CKV // MLA_KV_LORA)),
            pl.BlockSpec((bm, 2 * MLA_ROPE_DIM), lambda i: (i, COL_KR // (2 * MLA_ROPE_DIM))),
            pl.BlockSpec((bm, MLA_ROPE_DIM), lambda i: (i, 0)),
            pl.BlockSpec((bm, MLA_ROPE_DIM), lambda i: (i, 0)),
            pl.BlockSpec((MLA_ROPE_DIM // 2, bm), lambda i: (0, i)),
            pl.BlockSpec((MLA_ROPE_DIM // 2, bm), lambda i: (0, i)),
            full((1, MLA_Q_LORA)),
            full((1, MLA_KV_LORA)),
            full((H, MLA_QK_DIM, MLA_Q_LORA)),
            full((MLA_KV_LORA, H * MLA_NOPE_DIM)),
            full((H, MLA_V_DIM, MLA_KV_LORA)),
        ],
        out_specs=[
            pl.BlockSpec((1, H, MLA_QK_DIM, bm), lambda i: (i // spb, 0, 0, i % spb)),
            pl.BlockSpec((1, H, bm, MLA_QK_DIM), lambda i: (i // spb, 0, i % spb, 0)),
            pl.BlockSpec((1, H, MLA_V_PAD, bm), lambda i: (i // spb, 0, 0, i % spb)),
        ],
        out_shape=[
            jax.ShapeDtypeStruct((batch, H, MLA_QK_DIM, seq), BF16),
            jax.ShapeDtypeStruct((batch, H, seq, MLA_QK_DIM), BF16),
            jax.ShapeDtypeStruct((batch, H, MLA_V_PAD, seq), BF16),
        ],
        compiler_params=_params("parallel"),
        name="mla_proj",
    )(h, h, h, cos2, sin2, cos_t, sin_t, qg, kvg, wq_t, wk, wv_t)


def _attn_kernel(q_t_ref, k_ref, v_t_ref, o_ref, s_ref, mx_ref, p_ref, alpha_ref, m_ref, acc_ref, *, tq):
    qi = pl.program_id(2)
    n_rt = tq // ATTN_TILE
    n_full = qi * n_rt
    half = ATTN_TILE // 2
    m_ref[...] = jnp.full(m_ref.shape, MASK_VALUE, F32)
    acc_ref[...] = jnp.zeros(acc_ref.shape, F32)

    def key_start(tile):
        return pl.multiple_of(tile * ATTN_TILE, ATTN_TILE)

    def scores(tile, slot, rt, on_diag):
        q_t = q_t_ref[0, 0, :, rt * ATTN_TILE:(rt + 1) * ATTN_TILE]
        mx = None
        for part in range(2):
            k = k_ref[0, 0, pl.ds(key_start(tile) + part * half, half), :]
            s = jnp.dot(k, q_t, preferred_element_type=F32)
            if on_diag is not False:
                kc = (lax.broadcasted_iota(jnp.int32, s.shape, 0) + part * half) // CHUNK
                qc = lax.broadcasted_iota(jnp.int32, s.shape, 1) // CHUNK
                min_lag = 0 if on_diag is True else jnp.where(on_diag, 0, -ATTN_TILE)
                s = jnp.where(qc - kc >= min_lag, s, MASK_VALUE)
            s_ref[slot, rt, part * half:(part + 1) * half, :] = s
            part_max = jnp.max(s, axis=0, keepdims=True)
            mx = part_max if mx is None else jnp.maximum(mx, part_max)
        mx_ref[slot, rt] = jnp.broadcast_to(mx, mx_ref.shape[2:])

    def softmax(slot, rt):
        m_old = m_ref[rt, 0:1, :]
        m_new = jnp.maximum(m_old, mx_ref[slot, rt, 0:1, :])
        for part in range(2):
            rows = slice(part * half, (part + 1) * half)
            p_ref[slot, rt, rows, :] = jnp.exp2(s_ref[slot, rt, rows, :] - m_new).astype(BF16)
        alpha_ref[slot, rt] = jnp.broadcast_to(jnp.exp2(m_old - m_new), alpha_ref.shape[2:])
        m_ref[rt] = jnp.broadcast_to(m_new, m_ref.shape[1:])

    def accumulate(tile, slot, rt):
        v_t = v_t_ref[0, 0, :, pl.ds(key_start(tile), ATTN_TILE)]
        pv = jnp.dot(v_t, p_ref[slot, rt], preferred_element_type=F32)
        acc_ref[rt] = alpha_ref[slot, rt, 0:1, :] * acc_ref[rt] + pv

    def step(tile, slot, diag, next_is_first_overlap):
        if diag is None:
            cur, nxt = range(n_rt), range(n_rt)
        else:
            cur, nxt = range(diag, n_rt), range(diag + 1, n_rt)
        if diag is None:
            for rt in range(n_rt):
                softmax(slot, rt)
                scores(tile + 1, 1 - slot, rt, next_is_first_overlap if rt == 0 else False)
                accumulate(tile, slot, rt)
        else:
            for rt in nxt:
                scores(tile + 1, 1 - slot, rt, rt == diag + 1)
            for rt in cur:
                softmax(slot, rt)
            for rt in cur:
                accumulate(tile, slot, rt)

    def step_group(tile, diag):
        for i in range(ATTN_GROUP):
            if diag is not None:
                step(tile + i, i % 2, diag + i, False)
            elif i < ATTN_GROUP - 1:
                step(tile + i, i % 2, None, False)
            else:
                step(tile + i, i % 2, None, tile + ATTN_GROUP == n_full)

    for rt in range(n_rt):
        scores(0, 0, rt, (qi == 0) if rt == 0 else False)

    def body(u, carry):
        t = ATTN_GROUP * u
        pl.when(t < n_full)(functools.partial(step_group, t, None))
        for d in range(0, n_rt, ATTN_GROUP):
            pl.when(t == n_full + d)(functools.partial(step_group, t, d))
        return carry

    assert n_rt % ATTN_GROUP == 0 and ATTN_GROUP % 2 == 0
    lax.fori_loop(0, (n_full + n_rt) // ATTN_GROUP, body, 0)
    for rt in range(n_rt):
        acc = acc_ref[rt]
        o_t = acc[:MLA_V_DIM] / acc[MLA_V_DIM:MLA_V_DIM + 1]
        o_ref[rt * ATTN_TILE:(rt + 1) * ATTN_TILE, :] = o_t.T.astype(BF16)


def mla_attention(q_t, k, v_t, tq=1024):
    batch, H, seq, _ = k.shape
    nq = seq // tq
    n_rt = tq // ATTN_TILE
    return pl.pallas_call(
        functools.partial(_attn_kernel, tq=tq),
        grid=(batch, H, nq),
        in_specs=[
            pl.BlockSpec((1, 1, MLA_QK_DIM, tq), lambda b, h, i: (b, h, 0, i)),
            pl.BlockSpec((1, 1, seq, MLA_QK_DIM), lambda b, h, i: (b, h, 0, 0)),
            pl.BlockSpec((1, 1, MLA_V_PAD, seq), lambda b, h, i: (b, h, 0, 0)),
        ],
        out_specs=pl.BlockSpec((tq, MLA_V_DIM), lambda b, h, i: (b * nq + i, h)),
        out_shape=jax.ShapeDtypeStruct((batch * seq, H * MLA_V_DIM), BF16),
        scratch_shapes=[
            pltpu.VMEM((2, n_rt, ATTN_TILE, ATTN_TILE), F32),
            pltpu.VMEM((2, n_rt, 8, ATTN_TILE), F32),
            pltpu.VMEM((2, n_rt, ATTN_TILE, ATTN_TILE), BF16),
            pltpu.VMEM((2, n_rt, 8, ATTN_TILE), F32),
            pltpu.VMEM((n_rt, 8, ATTN_TILE), F32),
            pltpu.VMEM((n_rt, MLA_V_PAD, ATTN_TILE), F32),
        ],
        compiler_params=_params("parallel", "parallel", "arbitrary"),
        name="mla_attention",
    )(q_t, k, v_t)


def _ret_kernel(lg_ref, rq_ref, rk_ref, rv_ref, rg_ref, cos_ref, sin_ref, gng_ref, gnb_ref,
                o_ref, state_ref, decay_ref, qdec_ref, kdec_ref, *, T):
    h = pl.program_id(1)
    c = pl.program_id(2)
    lg = lg_ref[h]
    half = RET_QK_DIM // 2

    @pl.when(c == 0)
    def _():
        state_ref[...] = jnp.zeros_like(state_ref)
        q_scale = RET_QK_DIM ** -0.5
        row = lax.broadcasted_iota(jnp.int32, (T, T), 0)
        col = lax.broadcasted_iota(jnp.int32, (T, T), 1)
        dist = jnp.abs(row - col).astype(F32)
        decay_ref[...] = jnp.where(row // CHUNK >= col // CHUNK, jnp.exp(lg * dist) * q_scale, 0.0)
        pos = lax.broadcasted_iota(jnp.int32, (T, 1), 0).astype(F32)
        qdec_ref[...] = jnp.exp(lg * (pos + 1.0)) * q_scale
        kdec_ref[...] = jnp.exp(lg * (T - 1.0 - pos))

    cos = cos_ref[...]
    sin = sin_ref[...]

    def rope(t):
        t1, t2 = t[:, :half], t[:, half:]
        return jnp.concatenate([t1 * cos - t2 * sin, t2 * cos + t1 * sin], axis=-1)

    q = rope(rq_ref[...].astype(F32))
    k = rope(rk_ref[...].astype(F32))
    v = rv_ref[...]

    qb = q.astype(BF16)
    s = lax.dot_general(qb, k.astype(BF16), (((1,), (1,)), ((), ())), preferred_element_type=F32)
    o = jnp.dot((s * decay_ref[...]).astype(BF16), v, preferred_element_type=F32)
    state = state_ref[...]
    o = o + jnp.dot((q * qdec_ref[...]).astype(BF16), state.astype(BF16), preferred_element_type=F32)
    kv = lax.dot_general((k * kdec_ref[...]).astype(BF16), v, (((0,), (0,)), ((), ())),
                         preferred_element_type=F32)
    state_ref[...] = state * jnp.exp(lg * T) + kv

    mu = jnp.mean(o, axis=-1, keepdims=True)
    d = o - mu
    var = jnp.mean(d * d, axis=-1, keepdims=True)
    o = d * lax.rsqrt(var + GN_EPS) * gng_ref[...] + gnb_ref[...]
    g = rg_ref[...].astype(F32)
    o_ref[...] = (g * jax.nn.sigmoid(g) * o).astype(BF16)


def retention(h, log_gamma, cos_r, sin_r, gn_g, gn_b, batch, seq, T=512):
    n = h.shape[0]
    H = RET_HEADS
    nc = seq // T
    W = RET_QK_DIM

    def col(base):
        return pl.BlockSpec((T, W), lambda b, hh, c: (b * nc + c, base // W + hh))

    tab = pl.BlockSpec((T, W // 2), lambda b, hh, c: (b * nc + c, 0))
    vec = pl.BlockSpec((1, W), lambda b, hh, c: (0, hh))
    return pl.pallas_call(
        functools.partial(_ret_kernel, T=T),
        grid=(batch, H, nc),
        in_specs=[pl.BlockSpec(memory_space=pltpu.SMEM),
                  col(COL_RQ), col(COL_RK), col(COL_RV), col(COL_RG), tab, tab, vec, vec],
        out_specs=pl.BlockSpec((T, W), lambda b, hh, c: (b * nc + c, hh)),
        out_shape=jax.ShapeDtypeStruct((n, H * RET_V_DIM), BF16),
        scratch_shapes=[pltpu.VMEM((RET_QK_DIM, RET_V_DIM), F32), pltpu.VMEM((T, T), F32),
                        pltpu.VMEM((T, 1), F32), pltpu.VMEM((T, 1), F32)],
        compiler_params=_params("parallel", "parallel", "arbitrary"),
        name="retention",
    )(log_gamma, h, h, h, h, cos_r, sin_r, gn_g.reshape(1, -1), gn_b.reshape(1, -1))


def _out_ln_kernel(a_ref, r_ref, wa_ref, wr_ref, x_ref, g_ref, b_ref, o_ref, *, alpha):
    for s in range(a_ref.shape[0] // OUT_SUB):
        rows = slice(s * OUT_SUB, (s + 1) * OUT_SUB)
        mix = jnp.dot(a_ref[rows, :], wa_ref[...], preferred_element_type=F32)
        mix = mix + jnp.dot(r_ref[rows, :], wr_ref[...], preferred_element_type=F32)
        o_ref[rows, :] = _layer_norm(alpha * x_ref[rows, :] + mix, g_ref[...], b_ref[...])


def out_proj_ln(a, r, w, layer, x, g, b, alpha, bm=1024):
    n, d = x.shape
    ka, kr = a.shape[1], r.shape[1]
    assert ka == kr
    row = pl.BlockSpec((bm, d), lambda i: (i, 0))
    vec = pl.BlockSpec((1, d), lambda i: (0, 0))
    once = pl.Buffered(1)
    return pl.pallas_call(
        functools.partial(_out_ln_kernel, alpha=alpha),
        grid=(n // bm,),
        in_specs=[pl.BlockSpec((bm, ka), lambda i: (i, 0)),
                  pl.BlockSpec((bm, kr), lambda i: (i, 0)),
                  pl.BlockSpec((None, ka, d), lambda i: (layer, 0, 0), pipeline_mode=once),
                  pl.BlockSpec((None, kr, d), lambda i: (layer, 1, 0), pipeline_mode=once),
                  row, vec, vec],
        out_specs=row,
        out_shape=jax.ShapeDtypeStruct((n, d), F32),
        compiler_params=_params("parallel"),
        name="out_proj_ln",
    )(a, r, w, w, x, g.reshape(1, d), b.reshape(1, d))


def _ffn_ln_kernel(x_ref, wg_ref, wu_ref, wd_ref, g_ref, b_ref, o_ref, xb_ref, *, alpha):
    j = pl.program_id(1)
    last = pl.num_programs(1) - 1
    row_tiles = [slice(r, r + FFN_ROWS) for r in range(0, o_ref.shape[0], FFN_ROWS)]

    def hidden(xb):
        parts = []
        for s in range(wg_ref.shape[1] // FFN_SUB):
            cols = slice(s * FFN_SUB, (s + 1) * FFN_SUB)
            g = jnp.dot(xb, wg_ref[:, cols], preferred_element_type=F32)
            u = jnp.dot(xb, wu_ref[:, cols], preferred_element_type=F32)
            parts.append((g * jax.nn.sigmoid(g) * u).astype(BF16))
        return parts[0] if len(parts) == 1 else jnp.concatenate(parts, axis=1)

    @pl.when(j == 0)
    def _():
        for rows in row_tiles:
            xb = x_ref[rows, :].astype(BF16)
            xb_ref[rows, :] = xb
            o_ref[rows, :] = jnp.dot(hidden(xb), wd_ref[...], preferred_element_type=F32)

    @pl.when(jnp.logical_and(j > 0, j < last))
    def _():
        hm = hidden(xb_ref[...])
        for c in range(o_ref.shape[1] // FFN_OUT_CHUNK):
            cols = slice(c * FFN_OUT_CHUNK, (c + 1) * FFN_OUT_CHUNK)
            o_ref[:, cols] += jnp.dot(hm, wd_ref[:, cols], preferred_element_type=F32)

    @pl.when(j == last)
    def _():
        for rows in row_tiles:
            f = o_ref[rows, :] + jnp.dot(hidden(xb_ref[rows, :]), wd_ref[...], preferred_element_type=F32)
            o_ref[rows, :] = _layer_norm(alpha * x_ref[rows, :] + f, g_ref[...], b_ref[...])


def ffn_ln(x, wg, wu, wd, layer, g, b, alpha, bm=1024, bn=512):
    n, d = x.shape
    dff = wg.shape[2]
    row = pl.BlockSpec((bm, d), lambda i, j: (i, 0))
    vec = pl.BlockSpec((1, d), lambda i, j: (0, 0))
    wcol = pl.BlockSpec((None, d, bn), lambda i, j: (layer, 0, j))
    return pl.pallas_call(
        functools.partial(_ffn_ln_kernel, alpha=alpha),
        grid=(n // bm, dff // bn),
        in_specs=[row, wcol, wcol, pl.BlockSpec((None, bn, d), lambda i, j: (layer, j, 0)), vec, vec],
        out_specs=row,
        out_shape=jax.ShapeDtypeStruct((n, d), F32),
        scratch_shapes=[pltpu.VMEM((bm, d), BF16)],
        compiler_params=_params("parallel", "arbitrary"),
        name="ffn_ln",
    )(x, wg, wu, wd, g.reshape(1, d), b.reshape(1, d))


def _rotate_half_cols(w):
    half = w.shape[-1] // 2
    return jnp.concatenate([-w[..., half:], w[..., :half]], axis=-1)


def _prep_w_in(w):
    w_mla = w[..., :W_IN_MLA_COLS].astype(BF16)
    kr = w_mla[..., COL_KR:]
    pad = jnp.zeros(w.shape[:-1] + (MLA_IN_WIDTH - W_IN_MLA_COLS - MLA_ROPE_DIM,), BF16)
    w_mla = jnp.concatenate([w_mla, _rotate_half_cols(kr), pad], axis=-1)
    return w_mla, w[..., W_IN_MLA_COLS:].astype(BF16)


def _prep_w_uq(w):
    return w.astype(BF16).reshape(MLA_Q_LORA, MLA_HEADS, MLA_QK_DIM).transpose(1, 2, 0)


def _prep_w_ukv(w):
    w = w.astype(BF16).reshape(MLA_KV_LORA, MLA_HEADS, MLA_NOPE_DIM + MLA_V_DIM)
    wk = w[..., :MLA_NOPE_DIM].reshape(MLA_KV_LORA, MLA_HEADS * MLA_NOPE_DIM)
    return wk, w[..., MLA_NOPE_DIM:].transpose(1, 2, 0)


def _rope_tables(positions, dim):
    inv_freq = ROPE_THETA ** (-jnp.arange(0, dim, 2, dtype=F32) / dim)
    ang = positions.astype(F32)[..., None] * inv_freq
    return jnp.cos(ang), jnp.sin(ang)


def kernel(x, positions, ln_in_g, ln_in_b, w_in, q_norm_g, kv_norm_g, w_uq, w_ukv, ret_gn_g, ret_gn_b, w_out, ln1_g, ln1_b, w_gate, w_up, w_down, ln2_g, ln2_b):
    batch, seq, d = x.shape
    depth = w_in.shape[0]
    n = batch * seq
    alpha = (2 * depth) ** 0.25

    cos_m, sin_m = _rope_tables(positions, MLA_ROPE_DIM)
    cos_r, sin_r = _rope_tables(positions, RET_QK_DIM)
    cos2 = jnp.concatenate([cos_m, cos_m], axis=-1).reshape(n, MLA_ROPE_DIM)
    sin2 = jnp.concatenate([sin_m, sin_m], axis=-1).reshape(n, MLA_ROPE_DIM)
    cos_m_t, sin_m_t = cos_m.reshape(n, -1).T, sin_m.reshape(n, -1).T
    cos_r = cos_r.reshape(n, RET_QK_DIM // 2)
    sin_r = sin_r.reshape(n, RET_QK_DIM // 2)
    log_gamma = jnp.log1p(-jnp.exp2(-5.0 - jnp.arange(RET_HEADS, dtype=F32)))

    w_in_mla, w_in_ret = _prep_w_in(w_in)
    w_out_b = w_out.astype(BF16)
    w_gate_b, w_up_b, w_down_b = w_gate.astype(BF16), w_up.astype(BF16), w_down.astype(BF16)

    xf = ln_in(x.reshape(n, d), ln_in_g, ln_in_b)
    for l in range(depth):
        h_mla, h_ret = in_proj(xf, w_in_mla, w_in_ret, l)
        wk, wv_t = _prep_w_ukv(w_ukv[l])
        q_t, k, v_t = mla_proj(h_mla, cos2, sin2, cos_m_t, sin_m_t, q_norm_g[l].reshape(1, -1),
                               kv_norm_g[l].reshape(1, -1), _prep_w_uq(w_uq[l]), wk, wv_t, batch, seq)
        a = mla_attention(q_t, k, v_t)
        r = retention(h_ret, log_gamma, cos_r, sin_r, ret_gn_g[l], ret_gn_b[l], batch, seq)
        xf = out_proj_ln(a, r, w_out_b, l, xf, ln1_g[l], ln1_b[l], alpha)
        xf = ffn_ln(xf, w_gate_b, w_up_b, w_down_b, l, ln2_g[l], ln2_b[l], alpha)
    return xf.reshape(batch, seq, d)
```

```python
import functools

import jax
import jax.numpy as jnp
from jax import lax
from jax.experimental import pallas as pl
from jax.experimental.pallas import tpu as pltpu

F32 = jnp.float32
BF16 = jnp.bfloat16

D_MODEL = 2048
CHUNK = 64
MLA_HEADS = 8
MLA_Q_LORA = 512
MLA_KV_LORA = 256
MLA_NOPE_DIM = 128
MLA_ROPE_DIM = 64
MLA_V_DIM = 128
MLA_QK_DIM = MLA_NOPE_DIM + MLA_ROPE_DIM
RET_HEADS = 4
RET_QK_DIM = 256
RET_V_DIM = 256
ROPE_THETA = 10000.0
LN_EPS = 1e-5
RMS_EPS = 1e-6
GN_EPS = 1e-5

COL_CQ = 0
COL_CKV = 512
COL_KR = 768
MLA_IN_WIDTH = 1024
COL_RQ = 0
COL_RK = 1024
COL_RV = 2048
COL_RG = 3072
W_IN_MLA_COLS = 832

VMEM_LIMIT = 60 * 1024 * 1024

MASK_VALUE = -1e30
LOG2E = 1.4426950408889634
FFN_SUB = 256
FFN_OUT_CHUNK = 512
FFN_ROWS = 256
OUT_SUB = 256
ATTN_TILE = 256
ATTN_GROUP = 4
MLA_V_PAD = MLA_V_DIM + 16


def _params(*sem):
    return pltpu.CompilerParams(dimension_semantics=sem, vmem_limit_bytes=VMEM_LIMIT)


def _layer_norm(y, g, b):
    mu = jnp.mean(y, axis=-1, keepdims=True)
    d = y - mu
    var = jnp.mean(d * d, axis=-1, keepdims=True)
    return d * lax.rsqrt(var + LN_EPS) * g + b


def _ln_in_kernel(x_ref, g_ref, b_ref, o_ref):
    o_ref[...] = _layer_norm(x_ref[...], g_ref[...], b_ref[...])


def ln_in(x, g, b, bm=512):
    n, d = x.shape
    row = pl.BlockSpec((bm, d), lambda i: (i, 0))
    vec = pl.BlockSpec((1, d), lambda i: (0, 0))
    return pl.pallas_call(
        _ln_in_kernel,
        grid=(n // bm,),
        in_specs=[row, vec, vec],
        out_specs=row,
        out_shape=jax.ShapeDtypeStruct((n, d), F32),
        compiler_params=_params("parallel"),
        name="ln_in",
    )(x, g.reshape(1, d), b.reshape(1, d))


def _in_proj_kernel(x_ref, wm_ref, wr_ref, cos_ref, sin_ref, om_ref, or_ref, xb_ref):
    j = pl.program_id(1)
    half = RET_QK_DIM // 2
    heads = [slice(c, c + RET_QK_DIM) for c in range(0, or_ref.shape[1], RET_QK_DIM)]

    def head_dot(cols):
        return jnp.dot(xb_ref[...], wr_ref[:, cols], preferred_element_type=F32)

    @pl.when(j == 0)
    def _():
        xb_ref[...] = x_ref[...].astype(BF16)
        om_ref[...] = jnp.dot(xb_ref[...], wm_ref[...], preferred_element_type=F32).astype(BF16)

    @pl.when(jnp.logical_or(j == 1, j == 2))
    def _():
        cos, sin = cos_ref[...], sin_ref[...]
        for cols in heads:
            y = head_dot(cols)
            t1, t2 = y[:, :half], y[:, half:]
            or_ref[:, cols.start:cols.start + half] = (t1 * cos - t2 * sin).astype(BF16)
            or_ref[:, cols.start + half:cols.stop] = (t2 * cos + t1 * sin).astype(BF16)

    @pl.when(j == 3)
    def _():
        or_ref[...] = jnp.dot(xb_ref[...], wr_ref[...], preferred_element_type=F32).astype(BF16)

    @pl.when(j == 4)
    def _():
        for cols in heads:
            y = head_dot(cols)
            or_ref[:, cols] = (y * jax.nn.sigmoid(y)).astype(BF16)


def in_proj(x, w_mla, w_ret, cos_r, sin_r, layer, bm=1024):
    m, k = x.shape
    bn = w_mla.shape[2]
    n_ret = w_ret.shape[2]
    assert n_ret == 4 * bn == 4 * RET_HEADS * RET_QK_DIM
    tab = pl.BlockSpec((bm, RET_QK_DIM // 2), lambda i, j: (i, 0))
    return pl.pallas_call(
        _in_proj_kernel,
        grid=(m // bm, 1 + n_ret // bn),
        in_specs=[pl.BlockSpec((bm, k), lambda i, j: (i, 0)),
                  pl.BlockSpec((None, k, bn), lambda i, j: (layer, 0, 0)),
                  pl.BlockSpec((None, k, bn), lambda i, j: (layer, 0, jnp.maximum(j - 1, 0))),
                  tab, tab],
        out_specs=[pl.BlockSpec((bm, bn), lambda i, j: (i, 0)),
                   pl.BlockSpec((bm, bn), lambda i, j: (i, jnp.maximum(j - 1, 0)))],
        out_shape=[jax.ShapeDtypeStruct((m, bn), BF16), jax.ShapeDtypeStruct((m, n_ret), BF16)],
        scratch_shapes=[pltpu.VMEM((bm, k), BF16)],
        compiler_params=_params("parallel", "arbitrary"),
        name="in_proj",
    )(x, w_mla, w_ret, cos_r, sin_r)


def _rms_norm(t, g):
    return t * lax.rsqrt(jnp.mean(t * t, axis=-1, keepdims=True) + RMS_EPS) * g


def _mla_proj_kernel(cq_ref, ckv_ref, kr_ref, cos_ref, sin_ref, cos_t_ref, sin_t_ref, qg_ref, kvg_ref,
                     wq_t_ref, wk_ref, wv_t_ref, q_t_ref, k_ref, v_t_ref, *, q_scale):
    cqn = _rms_norm(cq_ref[...].astype(F32), qg_ref[...])
    ckvn = _rms_norm(ckv_ref[...].astype(F32), kvg_ref[...])
    cqn_t = cqn.T.astype(BF16)
    ckvn_t = ckvn.T.astype(BF16)
    ckvn = ckvn.astype(BF16)
    kr = kr_ref[...].astype(F32)
    k_rope = (kr[:, :MLA_ROPE_DIM] * cos_ref[...] + kr[:, MLA_ROPE_DIM:] * sin_ref[...]).astype(BF16)
    cos_t = cos_t_ref[...]
    sin_t = sin_t_ref[...]
    ones = jnp.ones((MLA_V_PAD - MLA_V_DIM, cqn_t.shape[1]), BF16)
    k_nope = jnp.dot(ckvn, wk_ref[...], preferred_element_type=F32).astype(BF16)
    rope_mid = MLA_NOPE_DIM + MLA_ROPE_DIM // 2
    for h in range(MLA_HEADS):
        qh = jnp.dot(wq_t_ref[h], cqn_t, preferred_element_type=F32)
        q_t_ref[0, h, :MLA_NOPE_DIM, :] = (qh[:MLA_NOPE_DIM] * q_scale).astype(BF16)
        r1, r2 = qh[MLA_NOPE_DIM:rope_mid], qh[rope_mid:]
        q_t_ref[0, h, MLA_NOPE_DIM:rope_mid, :] = ((r1 * cos_t - r2 * sin_t) * q_scale).astype(BF16)
        q_t_ref[0, h, rope_mid:, :] = ((r2 * cos_t + r1 * sin_t) * q_scale).astype(BF16)
        k_ref[0, h, :, :MLA_NOPE_DIM] = k_nope[:, h * MLA_NOPE_DIM:(h + 1) * MLA_NOPE_DIM]
        k_ref[0, h, :, MLA_NOPE_DIM:] = k_rope
        v_t_ref[0, h, :MLA_V_DIM, :] = jnp.dot(wv_t_ref[h], ckvn_t, preferred_element_type=F32).astype(BF16)
        v_t_ref[0, h, MLA_V_DIM:, :] = ones


def mla_proj(h, cos2, sin2, cos_t, sin_t, qg, kvg, wq_t, wk, wv_t, batch, seq, bm=512):
    n = h.shape[0]
    spb = seq // bm
    H = MLA_HEADS
    q_scale = (MLA_QK_DIM ** -0.5) * LOG2E
    full = lambda shape: pl.BlockSpec(shape, lambda i: (0,) * len(shape))
    return pl.pallas_call(
        functools.partial(_mla_proj_kernel, q_scale=q_scale),
        grid=(n // bm,),
        in_specs=[
            pl.BlockSpec((bm, MLA_Q_LORA), lambda i: (i, COL_CQ // MLA_Q_LORA)),
            pl.BlockSpec((bm, MLA_KV_LORA), lambda i: (i, COL_CKV // MLA_KV_LORA)),
            pl.BlockSpec((bm, 2 * MLA_ROPE_DIM), lambda i: (i, COL_KR // (2 * MLA_ROPE_DIM))),
            pl.BlockSpec((bm, MLA_ROPE_DIM), lambda i: (i, 0)),
            pl.BlockSpec((bm, MLA_ROPE_DIM), lambda i: (i, 0)),
            pl.BlockSpec((MLA_ROPE_DIM // 2, bm), lambda i: (0, i)),
            pl.BlockSpec((MLA_ROPE_DIM // 2, bm), lambda i: (0, i)),
            full((1, MLA_Q_LORA)),
            full((1, MLA_KV_LORA)),
            full((H, MLA_QK_DIM, MLA_Q_LORA)),
            full((MLA_KV_LORA, H * MLA_NOPE_DIM)),
            full((H, MLA_V_DIM, MLA_KV_LORA)),
        ],
        out_specs=[
            pl.BlockSpec((1, H, MLA_QK_DIM, bm), lambda i: (i // spb, 0, 0, i % spb)),
            pl.BlockSpec((1, H, bm, MLA_QK_DIM), lambda i: (i // spb, 0, i % spb, 0)),
            pl.BlockSpec((1, H, MLA_V_PAD, bm), lambda i: (i // spb, 0, 0, i % spb)),
        ],
        out_shape=[
            jax.ShapeDtypeStruct((batch, H, MLA_QK_DIM, seq), BF16),
            jax.ShapeDtypeStruct((batch, H, seq, MLA_QK_DIM), BF16),
            jax.ShapeDtypeStruct((batch, H, MLA_V_PAD, seq), BF16),
        ],
        compiler_params=_params("parallel"),
        name="mla_proj",
    )(h, h, h, cos2, sin2, cos_t, sin_t, qg, kvg, wq_t, wk, wv_t)


def _attn_kernel(q_t_ref, k_ref, v_t_ref, o_ref, s_ref, mx_ref, p_ref, alpha_ref, m_ref, acc_ref, *, tq):
    qi = pl.program_id(2)
    n_rt = tq // ATTN_TILE
    n_full = qi * n_rt
    half = ATTN_TILE // 2
    m_ref[...] = jnp.full(m_ref.shape, MASK_VALUE, F32)
    acc_ref[...] = jnp.zeros(acc_ref.shape, F32)

    def key_start(tile):
        return pl.multiple_of(tile * ATTN_TILE, ATTN_TILE)

    def scores(tile, slot, rt, on_diag):
        q_t = q_t_ref[0, 0, :, rt * ATTN_TILE:(rt + 1) * ATTN_TILE]
        mx = None
        for part in range(2):
            k = k_ref[0, 0, pl.ds(key_start(tile) + part * half, half), :]
            s = jnp.dot(k, q_t, preferred_element_type=F32)
            if on_diag is not False:
                kc = (lax.broadcasted_iota(jnp.int32, s.shape, 0) + part * half) // CHUNK
                qc = lax.broadcasted_iota(jnp.int32, s.shape, 1) // CHUNK
                min_lag = 0 if on_diag is True else jnp.where(on_diag, 0, -ATTN_TILE)
                s = jnp.where(qc - kc >= min_lag, s, MASK_VALUE)
            s_ref[slot, rt, part * half:(part + 1) * half, :] = s
            part_max = jnp.max(s, axis=0, keepdims=True)
            mx = part_max if mx is None else jnp.maximum(mx, part_max)
        mx_ref[slot, rt] = jnp.broadcast_to(mx, mx_ref.shape[2:])

    def softmax(slot, rt):
        m_old = m_ref[rt, 0:1, :]
        m_new = jnp.maximum(m_old, mx_ref[slot, rt, 0:1, :])
        for part in range(2):
            rows = slice(part * half, (part + 1) * half)
            p_ref[slot, rt, rows, :] = jnp.exp2(s_ref[slot, rt, rows, :] - m_new).astype(BF16)
        alpha_ref[slot, rt] = jnp.broadcast_to(jnp.exp2(m_old - m_new), alpha_ref.shape[2:])
        m_ref[rt] = jnp.broadcast_to(m_new, m_ref.shape[1:])

    def accumulate(tile, slot, rt):
        v_t = v_t_ref[0, 0, :, pl.ds(key_start(tile), ATTN_TILE)]
        pv = jnp.dot(v_t, p_ref[slot, rt], preferred_element_type=F32)
        acc_ref[rt] = alpha_ref[slot, rt, 0:1, :] * acc_ref[rt] + pv

    def step(tile, slot, diag, next_is_first_overlap):
        if diag is None:
            cur, nxt = range(n_rt), range(n_rt)
        else:
            cur, nxt = range(diag, n_rt), range(diag + 1, n_rt)
        if diag is None:
            for rt in range(n_rt):
                softmax(slot, rt)
                scores(tile + 1, 1 - slot, rt, next_is_first_overlap if rt == 0 else False)
                accumulate(tile, slot, rt)
        else:
            for rt in nxt:
                scores(tile + 1, 1 - slot, rt, rt == diag + 1)
            for rt in cur:
                softmax(slot, rt)
            for rt in cur:
                accumulate(tile, slot, rt)

    def step_group(tile, diag):
        for i in range(ATTN_GROUP):
            if diag is not None:
                step(tile + i, i % 2, diag + i, False)
            elif i < ATTN_GROUP - 1:
                step(tile + i, i % 2, None, False)
            else:
                step(tile + i, i % 2, None, tile + ATTN_GROUP == n_full)

    for rt in range(n_rt):
        scores(0, 0, rt, (qi == 0) if rt == 0 else False)

    def body(u, carry):
        t = ATTN_GROUP * u
        pl.when(t < n_full)(functools.partial(step_group, t, None))
        for d in range(0, n_rt, ATTN_GROUP):
            pl.when(t == n_full + d)(functools.partial(step_group, t, d))
        return carry

    assert n_rt % ATTN_GROUP == 0 and ATTN_GROUP % 2 == 0
    lax.fori_loop(0, (n_full + n_rt) // ATTN_GROUP, body, 0)
    for rt in range(n_rt):
        acc = acc_ref[rt]
        o_t = acc[:MLA_V_DIM] / acc[MLA_V_DIM:MLA_V_DIM + 1]
        o_ref[rt * ATTN_TILE:(rt + 1) * ATTN_TILE, :] = o_t.T.astype(BF16)


def mla_attention(q_t, k, v_t, tq=1024):
    batch, H, seq, _ = k.shape
    nq = seq // tq
    n_rt = tq // ATTN_TILE
    return pl.pallas_call(
        functools.partial(_attn_kernel, tq=tq),
        grid=(batch, H, nq),
        in_specs=[
            pl.BlockSpec((1, 1, MLA_QK_DIM, tq), lambda b, h, i: (b, h, 0, i)),
            pl.BlockSpec((1, 1, seq, MLA_QK_DIM), lambda b, h, i: (b, h, 0, 0)),
            pl.BlockSpec((1, 1, MLA_V_PAD, seq), lambda b, h, i: (b, h, 0, 0)),
        ],
        out_specs=pl.BlockSpec((tq, MLA_V_DIM), lambda b, h, i: (b * nq + i, h)),
        out_shape=jax.ShapeDtypeStruct((batch * seq, H * MLA_V_DIM), BF16),
        scratch_shapes=[
            pltpu.VMEM((2, n_rt, ATTN_TILE, ATTN_TILE), F32),
            pltpu.VMEM((2, n_rt, 8, ATTN_TILE), F32),
            pltpu.VMEM((2, n_rt, ATTN_TILE, ATTN_TILE), BF16),
            pltpu.VMEM((2, n_rt, 8, ATTN_TILE), F32),
            pltpu.VMEM((n_rt, 8, ATTN_TILE), F32),
            pltpu.VMEM((n_rt, MLA_V_PAD, ATTN_TILE), F32),
        ],
        compiler_params=_params("parallel", "parallel", "arbitrary"),
        name="mla_attention",
    )(q_t, k, v_t)


def _ret_kernel(lg_ref, rq_ref, rk_ref, rv_ref, rg_ref, gng_ref, gnb_ref,
                o_ref, state_ref, decay_ref, qdec_ref, kdec_ref, *, T):
    h = pl.program_id(1)
    c = pl.program_id(2)
    lg = lg_ref[h]

    @pl.when(c == 0)
    def _():
        state_ref[...] = jnp.zeros_like(state_ref)
        q_scale = RET_QK_DIM ** -0.5
        row = lax.broadcasted_iota(jnp.int32, (T, T), 0)
        col = lax.broadcasted_iota(jnp.int32, (T, T), 1)
        dist = jnp.abs(row - col).astype(F32)
        decay_ref[...] = jnp.where(row // CHUNK >= col // CHUNK, jnp.exp(lg * dist) * q_scale, 0.0)
        pos = lax.broadcasted_iota(jnp.int32, (T, 1), 0).astype(F32)
        qdec_ref[...] = jnp.exp(lg * (pos + 1.0)) * q_scale
        kdec_ref[...] = jnp.exp(lg * (T - 1.0 - pos))

    qb = rq_ref[...]
    kb = rk_ref[...]
    q = qb.astype(F32)
    k = kb.astype(F32)
    v = rv_ref[...]

    s = lax.dot_general(qb, kb, (((1,), (1,)), ((), ())), preferred_element_type=F32)
    o = jnp.dot((s * decay_ref[...]).astype(BF16), v, preferred_element_type=F32)
    state = state_ref[...]
    o = o + jnp.dot((q * qdec_ref[...]).astype(BF16), state.astype(BF16), preferred_element_type=F32)
    kv = lax.dot_general((k * kdec_ref[...]).astype(BF16), v, (((0,), (0,)), ((), ())),
                         preferred_element_type=F32)
    state_ref[...] = state * jnp.exp(lg * T) + kv

    mu = jnp.mean(o, axis=-1, keepdims=True)
    d = o - mu
    var = jnp.mean(d * d, axis=-1, keepdims=True)
    o = d * lax.rsqrt(var + GN_EPS) * gng_ref[...] + gnb_ref[...]
    o_ref[...] = (rg_ref[...].astype(F32) * o).astype(BF16)


def retention(h, log_gamma, gn_g, gn_b, batch, seq, T=512):
    n = h.shape[0]
    H = RET_HEADS
    nc = seq // T
    W = RET_QK_DIM

    def col(base):
        return pl.BlockSpec((T, W), lambda b, hh, c: (b * nc + c, base // W + hh))

    vec = pl.BlockSpec((1, W), lambda b, hh, c: (0, hh))
    return pl.pallas_call(
        functools.partial(_ret_kernel, T=T),
        grid=(batch, H, nc),
        in_specs=[pl.BlockSpec(memory_space=pltpu.SMEM),
                  col(COL_RQ), col(COL_RK), col(COL_RV), col(COL_RG), vec, vec],
        out_specs=pl.BlockSpec((T, W), lambda b, hh, c: (b * nc + c, hh)),
        out_shape=jax.ShapeDtypeStruct((n, H * RET_V_DIM), BF16),
        scratch_shapes=[pltpu.VMEM((RET_QK_DIM, RET_V_DIM), F32), pltpu.VMEM((T, T), F32),
                        pltpu.VMEM((T, 1), F32), pltpu.VMEM((T, 1), F32)],
        compiler_params=_params("parallel", "parallel", "arbitrary"),
        name="retention",
    )(log_gamma, h, h, h, h, gn_g.reshape(1, -1), gn_b.reshape(1, -1))


def _out_ln_kernel(a_ref, r_ref, wa_ref, wr_ref, x_ref, g_ref, b_ref, o_ref, *, alpha):
    for s in range(a_ref.shape[0] // OUT_SUB):
        rows = slice(s * OUT_SUB, (s + 1) * OUT_SUB)
        mix = jnp.dot(a_ref[rows, :], wa_ref[...], preferred_element_type=F32)
        mix = mix + jnp.dot(r_ref[rows, :], wr_ref[...], preferred_element_type=F32)
        o_ref[rows, :] = _layer_norm(alpha * x_ref[rows, :] + mix, g_ref[...], b_ref[...])


def out_proj_ln(a, r, w, layer, x, g, b, alpha, bm=1024):
    n, d = x.shape
    ka, kr = a.shape[1], r.shape[1]
    assert ka == kr
    row = pl.BlockSpec((bm, d), lambda i: (i, 0))
    vec = pl.BlockSpec((1, d), lambda i: (0, 0))
    once = pl.Buffered(1)
    return pl.pallas_call(
        functools.partial(_out_ln_kernel, alpha=alpha),
        grid=(n // bm,),
        in_specs=[pl.BlockSpec((bm, ka), lambda i: (i, 0)),
                  pl.BlockSpec((bm, kr), lambda i: (i, 0)),
                  pl.BlockSpec((None, ka, d), lambda i: (layer, 0, 0), pipeline_mode=once),
                  pl.BlockSpec((None, kr, d), lambda i: (layer, 1, 0), pipeline_mode=once),
                  row, vec, vec],
        out_specs=row,
        out_shape=jax.ShapeDtypeStruct((n, d), F32),
        compiler_params=_params("parallel"),
        name="out_proj_ln",
    )(a, r, w, w, x, g.reshape(1, d), b.reshape(1, d))


def _ffn_ln_kernel(x_ref, wg_ref, wu_ref, wd_ref, g_ref, b_ref, o_ref, xb_ref, *, alpha):
    j = pl.program_id(1)
    last = pl.num_programs(1) - 1
    row_tiles = [slice(r, r + FFN_ROWS) for r in range(0, o_ref.shape[0], FFN_ROWS)]

    def hidden(xb):
        parts = []
        for s in range(wg_ref.shape[1] // FFN_SUB):
            cols = slice(s * FFN_SUB, (s + 1) * FFN_SUB)
            g = jnp.dot(xb, wg_ref[:, cols], preferred_element_type=F32)
            u = jnp.dot(xb, wu_ref[:, cols], preferred_element_type=F32)
            parts.append((g * jax.nn.sigmoid(g) * u).astype(BF16))
        return parts[0] if len(parts) == 1 else jnp.concatenate(parts, axis=1)

    @pl.when(j == 0)
    def _():
        for rows in row_tiles:
            xb = x_ref[rows, :].astype(BF16)
            xb_ref[rows, :] = xb
            o_ref[rows, :] = jnp.dot(hidden(xb), wd_ref[...], preferred_element_type=F32)

    @pl.when(jnp.logical_and(j > 0, j < last))
    def _():
        hm = hidden(xb_ref[...])
        for c in range(o_ref.shape[1] // FFN_OUT_CHUNK):
            cols = slice(c * FFN_OUT_CHUNK, (c + 1) * FFN_OUT_CHUNK)
            o_ref[:, cols] += jnp.dot(hm, wd_ref[:, cols], preferred_element_type=F32)

    @pl.when(j == last)
    def _():
        for rows in row_tiles:
            f = o_ref[rows, :] + jnp.dot(hidden(xb_ref[rows, :]), wd_ref[...], preferred_element_type=F32)
            o_ref[rows, :] = _layer_norm(alpha * x_ref[rows, :] + f, g_ref[...], b_ref[...])


def ffn_ln(x, wg, wu, wd, layer, g, b, alpha, bm=1024, bn=512):
    n, d = x.shape
    dff = wg.shape[2]
    row = pl.BlockSpec((bm, d), lambda i, j: (i, 0))
    vec = pl.BlockSpec((1, d), lambda i, j: (0, 0))
    wcol = pl.BlockSpec((None, d, bn), lambda i, j: (layer, 0, j))
    return pl.pallas_call(
        functools.partial(_ffn_ln_kernel, alpha=alpha),
        grid=(n // bm, dff // bn),
        in_specs=[row, wcol, wcol, pl.BlockSpec((None, bn, d), lambda i, j: (layer, j, 0)), vec, vec],
        out_specs=row,
        out_shape=jax.ShapeDtypeStruct((n, d), F32),
        scratch_shapes=[pltpu.VMEM((bm, d), BF16)],
        compiler_params=_params("parallel", "arbitrary"),
        name="ffn_ln",
    )(x, wg, wu, wd, g.reshape(1, d), b.reshape(1, d))


def _rotate_half_cols(w):
    half = w.shape[-1] // 2
    return jnp.concatenate([-w[..., half:], w[..., :half]], axis=-1)


def _prep_w_in(w):
    w_mla = w[..., :W_IN_MLA_COLS].astype(BF16)
    kr = w_mla[..., COL_KR:]
    pad = jnp.zeros(w.shape[:-1] + (MLA_IN_WIDTH - W_IN_MLA_COLS - MLA_ROPE_DIM,), BF16)
    w_mla = jnp.concatenate([w_mla, _rotate_half_cols(kr), pad], axis=-1)
    return w_mla, w[..., W_IN_MLA_COLS:].astype(BF16)


def _prep_w_uq(w):
    return w.astype(BF16).reshape(MLA_Q_LORA, MLA_HEADS, MLA_QK_DIM).transpose(1, 2, 0)


def _prep_w_ukv(w):
    w = w.astype(BF16).reshape(MLA_KV_LORA, MLA_HEADS, MLA_NOPE_DIM + MLA_V_DIM)
    wk = w[..., :MLA_NOPE_DIM].reshape(MLA_KV_LORA, MLA_HEADS * MLA_NOPE_DIM)
    return wk, w[..., MLA_NOPE_DIM:].transpose(1, 2, 0)


def _rope_tables(positions, dim):
    inv_freq = ROPE_THETA ** (-jnp.arange(0, dim, 2, dtype=F32) / dim)
    ang = positions.astype(F32)[..., None] * inv_freq
    return jnp.cos(ang), jnp.sin(ang)


def kernel(x, positions, ln_in_g, ln_in_b, w_in, q_norm_g, kv_norm_g, w_uq, w_ukv, ret_gn_g, ret_gn_b, w_out, ln1_g, ln1_b, w_gate, w_up, w_down, ln2_g, ln2_b):
    batch, seq, d = x.shape
    depth = w_in.shape[0]
    n = batch * seq
    alpha = (2 * depth) ** 0.25

    cos_m, sin_m = _rope_tables(positions, MLA_ROPE_DIM)
    cos_r, sin_r = _rope_tables(positions, RET_QK_DIM)
    cos2 = jnp.concatenate([cos_m, cos_m], axis=-1).reshape(n, MLA_ROPE_DIM)
    sin2 = jnp.concatenate([sin_m, sin_m], axis=-1).reshape(n, MLA_ROPE_DIM)
    cos_m_t, sin_m_t = cos_m.reshape(n, -1).T, sin_m.reshape(n, -1).T
    cos_r = cos_r.reshape(n, RET_QK_DIM // 2)
    sin_r = sin_r.reshape(n, RET_QK_DIM // 2)
    log_gamma = jnp.log1p(-jnp.exp2(-5.0 - jnp.arange(RET_HEADS, dtype=F32)))

    w_in_mla, w_in_ret = _prep_w_in(w_in)
    w_out_b = w_out.astype(BF16)
    w_gate_b, w_up_b, w_down_b = w_gate.astype(BF16), w_up.astype(BF16), w_down.astype(BF16)

    xf = ln_in(x.reshape(n, d), ln_in_g, ln_in_b)
    for l in range(depth):
        h_mla, h_ret = in_proj(xf, w_in_mla, w_in_ret, cos_r, sin_r, l)
        wk, wv_t = _prep_w_ukv(w_ukv[l])
        q_t, k, v_t = mla_proj(h_mla, cos2, sin2, cos_m_t, sin_m_t, q_norm_g[l].reshape(1, -1),
                               kv_norm_g[l].reshape(1, -1), _prep_w_uq(w_uq[l]), wk, wv_t, batch, seq)
        a = mla_attention(q_t, k, v_t)
        r = retention(h_ret, log_gamma, ret_gn_g[l], ret_gn_b[l], batch, seq)
        xf = out_proj_ln(a, r, w_out_b, l, xf, ln1_g[l], ln1_b[l], alpha)
        xf = ffn_ln(xf, w_gate_b, w_up_b, w_down_b, l, ln2_g[l], ln2_b[l], alpha)
    return xf.reshape(batch, seq, d)
```

```python
import functools

import jax
import jax.numpy as jnp
from jax import lax
from jax.experimental import pallas as pl
from jax.experimental.pallas import tpu as pltpu

F32 = jnp.float32
BF16 = jnp.bfloat16

D_MODEL = 2048
CHUNK = 64
MLA_HEADS = 8
MLA_Q_LORA = 512
MLA_KV_LORA = 256
MLA_NOPE_DIM = 128
MLA_ROPE_DIM = 64
MLA_V_DIM = 128
MLA_QK_DIM = MLA_NOPE_DIM + MLA_ROPE_DIM
RET_HEADS = 4
RET_QK_DIM = 256
RET_V_DIM = 256
ROPE_THETA = 10000.0
LN_EPS = 1e-5
RMS_EPS = 1e-6
GN_EPS = 1e-5

COL_CQ = 0
COL_CKV = 512
COL_KR = 768
MLA_IN_WIDTH = 1024
COL_RQ = 0
COL_RK = 1024
COL_RV = 2048
COL_RG = 3072
W_IN_MLA_COLS = 832

VMEM_LIMIT = 60 * 1024 * 1024

MASK_VALUE = -1e30
LOG2E = 1.4426950408889634
FFN_SUB = 256
FFN_OUT_CHUNK = 512
FFN_ROWS = 256
OUT_SUB = 256
ATTN_TILE = 256
ATTN_GROUP = 4
SUBLANES = 8
MLA_V_PAD = MLA_V_DIM + 2 * SUBLANES


def _params(*sem):
    return pltpu.CompilerParams(dimension_semantics=sem, vmem_limit_bytes=VMEM_LIMIT)


def _layer_norm(y, g, b):
    mu = jnp.mean(y, axis=-1, keepdims=True)
    d = y - mu
    var = jnp.mean(d * d, axis=-1, keepdims=True)
    return d * lax.rsqrt(var + LN_EPS) * g + b


def _ln_in_kernel(x_ref, g_ref, b_ref, o_ref):
    o_ref[...] = _layer_norm(x_ref[...], g_ref[...], b_ref[...])


def ln_in(x, g, b, bm=512):
    n, d = x.shape
    row = pl.BlockSpec((bm, d), lambda i: (i, 0))
    vec = pl.BlockSpec((1, d), lambda i: (0, 0))
    return pl.pallas_call(
        _ln_in_kernel,
        grid=(n // bm,),
        in_specs=[row, vec, vec],
        out_specs=row,
        out_shape=jax.ShapeDtypeStruct((n, d), F32),
        compiler_params=_params("parallel"),
        name="ln_in",
    )(x, g.reshape(1, d), b.reshape(1, d))


def _in_proj_kernel(x_ref, wm_ref, wr_ref, cos_ref, sin_ref, om_ref, or_ref, xb_ref):
    j = pl.program_id(1)
    half = RET_QK_DIM // 2
    heads = [slice(c, c + RET_QK_DIM) for c in range(0, or_ref.shape[1], RET_QK_DIM)]

    def head_dot(cols):
        return jnp.dot(xb_ref[...], wr_ref[:, cols], preferred_element_type=F32)

    @pl.when(j == 0)
    def _():
        xb_ref[...] = x_ref[...].astype(BF16)
        om_ref[...] = jnp.dot(xb_ref[...], wm_ref[...], preferred_element_type=F32).astype(BF16)

    @pl.when(jnp.logical_or(j == 1, j == 2))
    def _():
        cos, sin = cos_ref[...], sin_ref[...]
        for cols in heads:
            y = head_dot(cols)
            t1, t2 = y[:, :half], y[:, half:]
            or_ref[:, cols.start:cols.start + half] = (t1 * cos - t2 * sin).astype(BF16)
            or_ref[:, cols.start + half:cols.stop] = (t2 * cos + t1 * sin).astype(BF16)

    @pl.when(j == 3)
    def _():
        or_ref[...] = jnp.dot(xb_ref[...], wr_ref[...], preferred_element_type=F32).astype(BF16)

    @pl.when(j == 4)
    def _():
        for cols in heads:
            y = head_dot(cols)
            or_ref[:, cols] = (y * jax.nn.sigmoid(y)).astype(BF16)


def in_proj(x, w_mla, w_ret, cos_r, sin_r, layer, bm=1024):
    m, k = x.shape
    bn = w_mla.shape[2]
    n_ret = w_ret.shape[2]
    assert n_ret == 4 * bn == 4 * RET_HEADS * RET_QK_DIM
    tab = pl.BlockSpec((bm, RET_QK_DIM // 2), lambda i, j: (i, 0))
    return pl.pallas_call(
        _in_proj_kernel,
        grid=(m // bm, 1 + n_ret // bn),
        in_specs=[pl.BlockSpec((bm, k), lambda i, j: (i, 0)),
                  pl.BlockSpec((None, k, bn), lambda i, j: (layer, 0, 0)),
                  pl.BlockSpec((None, k, bn), lambda i, j: (layer, 0, jnp.maximum(j - 1, 0))),
                  tab, tab],
        out_specs=[pl.BlockSpec((bm, bn), lambda i, j: (i, 0)),
                   pl.BlockSpec((bm, bn), lambda i, j: (i, jnp.maximum(j - 1, 0)))],
        out_shape=[jax.ShapeDtypeStruct((m, bn), BF16), jax.ShapeDtypeStruct((m, n_ret), BF16)],
        scratch_shapes=[pltpu.VMEM((bm, k), BF16)],
        compiler_params=_params("parallel", "arbitrary"),
        name="in_proj",
    )(x, w_mla, w_ret, cos_r, sin_r)


def _rms_norm(t, g):
    return t * lax.rsqrt(jnp.mean(t * t, axis=-1, keepdims=True) + RMS_EPS) * g


def _mla_proj_kernel(cq_ref, ckv_ref, kr_ref, cos_ref, sin_ref, cos_t_ref, sin_t_ref, qg_ref, kvg_ref,
                     wq_t_ref, wk_ref, wv_t_ref, q_t_ref, k_ref, v_t_ref, *, q_scale):
    cqn = _rms_norm(cq_ref[...].astype(F32), qg_ref[...])
    ckvn = _rms_norm(ckv_ref[...].astype(F32), kvg_ref[...])
    cqn_t = cqn.T.astype(BF16)
    ckvn_t = ckvn.T.astype(BF16)
    ckvn = ckvn.astype(BF16)
    kr = kr_ref[...].astype(F32)
    k_rope = (kr[:, :MLA_ROPE_DIM] * cos_ref[...] + kr[:, MLA_ROPE_DIM:] * sin_ref[...]).astype(BF16)
    cos_t = cos_t_ref[...]
    sin_t = sin_t_ref[...]
    ones = jnp.ones((MLA_V_PAD - MLA_V_DIM, cqn_t.shape[1]), BF16)
    k_nope = jnp.dot(ckvn, wk_ref[...], preferred_element_type=F32).astype(BF16)
    rope_mid = MLA_NOPE_DIM + MLA_ROPE_DIM // 2
    for h in range(MLA_HEADS):
        qh = jnp.dot(wq_t_ref[h], cqn_t, preferred_element_type=F32)
        q_t_ref[0, h, :MLA_NOPE_DIM, :] = (qh[:MLA_NOPE_DIM] * q_scale).astype(BF16)
        r1, r2 = qh[MLA_NOPE_DIM:rope_mid], qh[rope_mid:]
        q_t_ref[0, h, MLA_NOPE_DIM:rope_mid, :] = ((r1 * cos_t - r2 * sin_t) * q_scale).astype(BF16)
        q_t_ref[0, h, rope_mid:, :] = ((r2 * cos_t + r1 * sin_t) * q_scale).astype(BF16)
        k_ref[0, h, :, :MLA_NOPE_DIM] = k_nope[:, h * MLA_NOPE_DIM:(h + 1) * MLA_NOPE_DIM]
        k_ref[0, h, :, MLA_NOPE_DIM:] = k_rope
        v_t_ref[0, h, :MLA_V_DIM, :] = jnp.dot(wv_t_ref[h], ckvn_t, preferred_element_type=F32).astype(BF16)
        v_t_ref[0, h, MLA_V_DIM:, :] = ones


def mla_proj(h, cos2, sin2, cos_t, sin_t, qg, kvg, wq_t, wk, wv_t, batch, seq, bm=1024):
    n = h.shape[0]
    spb = seq // bm
    H = MLA_HEADS
    q_scale = (MLA_QK_DIM ** -0.5) * LOG2E
    full = lambda shape: pl.BlockSpec(shape, lambda i: (0,) * len(shape))
    return pl.pallas_call(
        functools.partial(_mla_proj_kernel, q_scale=q_scale),
        grid=(n // bm,),
        in_specs=[
            pl.BlockSpec((bm, MLA_Q_LORA), lambda i: (i, COL_CQ // MLA_Q_LORA)),
            pl.BlockSpec((bm, MLA_KV_LORA), lambda i: (i, COL_CKV // MLA_KV_LORA)),
            pl.BlockSpec((bm, 2 * MLA_ROPE_DIM), lambda i: (i, COL_KR // (2 * MLA_ROPE_DIM))),
            pl.BlockSpec((bm, MLA_ROPE_DIM), lambda i: (i, 0)),
            pl.BlockSpec((bm, MLA_ROPE_DIM), lambda i: (i, 0)),
            pl.BlockSpec((MLA_ROPE_DIM // 2, bm), lambda i: (0, i)),
            pl.BlockSpec((MLA_ROPE_DIM // 2, bm), lambda i: (0, i)),
            full((1, MLA_Q_LORA)),
            full((1, MLA_KV_LORA)),
            full((H, MLA_QK_DIM, MLA_Q_LORA)),
            full((MLA_KV_LORA, H * MLA_NOPE_DIM)),
            full((H, MLA_V_DIM, MLA_KV_LORA)),
        ],
        out_specs=[
            pl.BlockSpec((1, H, MLA_QK_DIM, bm), lambda i: (i // spb, 0, 0, i % spb)),
            pl.BlockSpec((1, H, bm, MLA_QK_DIM), lambda i: (i // spb, 0, i % spb, 0)),
            pl.BlockSpec((1, H, MLA_V_PAD, bm), lambda i: (i // spb, 0, 0, i % spb)),
        ],
        out_shape=[
            jax.ShapeDtypeStruct((batch, H, MLA_QK_DIM, seq), BF16),
            jax.ShapeDtypeStruct((batch, H, seq, MLA_QK_DIM), BF16),
            jax.ShapeDtypeStruct((batch, H, MLA_V_PAD, seq), BF16),
        ],
        compiler_params=_params("parallel"),
        name="mla_proj",
    )(h, h, h, cos2, sin2, cos_t, sin_t, qg, kvg, wq_t, wk, wv_t)


def _attn_kernel(q_t_ref, k_ref, v_t_ref, o_ref, s_ref, mx_ref, p_ref, alpha_ref, m_ref, acc_ref, *, tq):
    qi = pl.program_id(2)
    n_rt = tq // ATTN_TILE
    n_full = qi * n_rt
    half = ATTN_TILE // 2
    m_ref[...] = jnp.full(m_ref.shape, MASK_VALUE, F32)
    acc_ref[...] = jnp.zeros(acc_ref.shape, F32)

    def key_start(tile):
        return pl.multiple_of(tile * ATTN_TILE, ATTN_TILE)

    def scores(tile, slot, rt, on_diag):
        q_t = q_t_ref[0, 0, :, rt * ATTN_TILE:(rt + 1) * ATTN_TILE]
        mx = None
        for part in range(2):
            k = k_ref[0, 0, pl.ds(key_start(tile) + part * half, half), :]
            s = jnp.dot(k, q_t, preferred_element_type=F32)
            if on_diag is not False:
                kc = (lax.broadcasted_iota(jnp.int32, s.shape, 0) + part * half) // CHUNK
                qc = lax.broadcasted_iota(jnp.int32, s.shape, 1) // CHUNK
                min_lag = 0 if on_diag is True else jnp.where(on_diag, 0, -ATTN_TILE)
                s = jnp.where(qc - kc >= min_lag, s, MASK_VALUE)
            s_ref[slot, rt, part * half:(part + 1) * half, :] = s
            part_max = jnp.max(s, axis=0, keepdims=True)
            mx = part_max if mx is None else jnp.maximum(mx, part_max)
        mx_ref[slot, rt] = jnp.broadcast_to(mx, mx_ref.shape[2:])

    def softmax(slot, rt):
        m_old = m_ref[rt, 0:1, :]
        m_new = jnp.maximum(m_old, mx_ref[slot, rt, 0:1, :])
        for part in range(2):
            rows = slice(part * half, (part + 1) * half)
            p_ref[slot, rt, rows, :] = jnp.exp2(s_ref[slot, rt, rows, :] - m_new).astype(BF16)
        alpha_ref[slot, rt] = jnp.broadcast_to(jnp.exp2(m_old - m_new), alpha_ref.shape[2:])
        m_ref[rt] = jnp.broadcast_to(m_new, m_ref.shape[1:])

    def accumulate(tile, slot, rt):
        v_t = v_t_ref[0, 0, :, pl.ds(key_start(tile), ATTN_TILE)]
        pv = jnp.dot(v_t, p_ref[slot, rt], preferred_element_type=F32)
        acc_ref[rt] = alpha_ref[slot, rt, 0:1, :] * acc_ref[rt] + pv

    def step(tile, slot, diag, next_is_first_overlap):
        if diag is None:
            cur, nxt = range(n_rt), range(n_rt)
        else:
            cur, nxt = range(diag, n_rt), range(diag + 1, n_rt)
        if diag is None:
            for rt in range(n_rt):
                softmax(slot, rt)
                scores(tile + 1, 1 - slot, rt, next_is_first_overlap if rt == 0 else False)
                accumulate(tile, slot, rt)
        else:
            for rt in nxt:
                scores(tile + 1, 1 - slot, rt, rt == diag + 1)
            for rt in cur:
                softmax(slot, rt)
            for rt in cur:
                accumulate(tile, slot, rt)

    def step_group(tile, diag):
        for i in range(ATTN_GROUP):
            if diag is not None:
                step(tile + i, i % 2, diag + i, False)
            elif i < ATTN_GROUP - 1:
                step(tile + i, i % 2, None, False)
            else:
                step(tile + i, i % 2, None, tile + ATTN_GROUP == n_full)

    for rt in range(n_rt):
        scores(0, 0, rt, (qi == 0) if rt == 0 else False)

    def body(u, carry):
        t = ATTN_GROUP * u
        pl.when(t < n_full)(functools.partial(step_group, t, None))
        for d in range(0, n_rt, ATTN_GROUP):
            pl.when(t == n_full + d)(functools.partial(step_group, t, d))
        return carry

    assert n_rt % ATTN_GROUP == 0 and ATTN_GROUP % 2 == 0
    lax.fori_loop(0, (n_full + n_rt) // ATTN_GROUP, body, 0)
    for rt in range(n_rt):
        acc = acc_ref[rt]
        o_t = acc[:MLA_V_DIM] / acc[MLA_V_DIM:MLA_V_DIM + 1]
        o_ref[rt * ATTN_TILE:(rt + 1) * ATTN_TILE, :] = o_t.T.astype(BF16)


def mla_attention(q_t, k, v_t, tq=1024):
    batch, H, seq, _ = k.shape
    nq = seq // tq
    n_rt = tq // ATTN_TILE
    return pl.pallas_call(
        functools.partial(_attn_kernel, tq=tq),
        grid=(batch, H, nq),
        in_specs=[
            pl.BlockSpec((1, 1, MLA_QK_DIM, tq), lambda b, h, i: (b, h, 0, i)),
            pl.BlockSpec((1, 1, seq, MLA_QK_DIM), lambda b, h, i: (b, h, 0, 0)),
            pl.BlockSpec((1, 1, MLA_V_PAD, seq), lambda b, h, i: (b, h, 0, 0)),
        ],
        out_specs=pl.BlockSpec((tq, MLA_V_DIM), lambda b, h, i: (b * nq + i, h)),
        out_shape=jax.ShapeDtypeStruct((batch * seq, H * MLA_V_DIM), BF16),
        scratch_shapes=[
            pltpu.VMEM((2, n_rt, ATTN_TILE, ATTN_TILE), F32),
            pltpu.VMEM((2, n_rt, SUBLANES, ATTN_TILE), F32),
            pltpu.VMEM((2, n_rt, ATTN_TILE, ATTN_TILE), BF16),
            pltpu.VMEM((2, n_rt, SUBLANES, ATTN_TILE), F32),
            pltpu.VMEM((n_rt, SUBLANES, ATTN_TILE), F32),
            pltpu.VMEM((n_rt, MLA_V_PAD, ATTN_TILE), F32),
        ],
        compiler_params=_params("parallel", "parallel", "arbitrary"),
        name="mla_attention",
    )(q_t, k, v_t)


def _ret_kernel(lg_ref, rq_ref, rk_ref, rv_ref, rg_ref, gng_ref, gnb_ref,
                o_ref, state_ref, decay_ref, qdec_ref, kdec_ref, *, T):
    h = pl.program_id(1)
    c = pl.program_id(2)
    lg = lg_ref[h]

    @pl.when(c == 0)
    def _():
        state_ref[...] = jnp.zeros_like(state_ref)
        q_scale = RET_QK_DIM ** -0.5
        row = lax.broadcasted_iota(jnp.int32, (T, T), 0)
        col = lax.broadcasted_iota(jnp.int32, (T, T), 1)
        dist = jnp.abs(row - col).astype(F32)
        decay_ref[...] = jnp.where(row // CHUNK >= col // CHUNK, jnp.exp(lg * dist) * q_scale, 0.0)
        pos = lax.broadcasted_iota(jnp.int32, (T, 1), 0).astype(F32)
        qdec_ref[...] = jnp.exp(lg * (pos + 1.0)) * q_scale
        kdec_ref[...] = jnp.exp(lg * (T - 1.0 - pos))

    qb = rq_ref[...]
    kb = rk_ref[...]
    q = qb.astype(F32)
    k = kb.astype(F32)
    v = rv_ref[...]

    s = lax.dot_general(qb, kb, (((1,), (1,)), ((), ())), preferred_element_type=F32)
    o = jnp.dot((s * decay_ref[...]).astype(BF16), v, preferred_element_type=F32)
    state = state_ref[...]
    o = o + jnp.dot((q * qdec_ref[...]).astype(BF16), state.astype(BF16), preferred_element_type=F32)
    kv = lax.dot_general((k * kdec_ref[...]).astype(BF16), v, (((0,), (0,)), ((), ())),
                         preferred_element_type=F32)
    state_ref[...] = state * jnp.exp(lg * T) + kv

    mu = jnp.mean(o, axis=-1, keepdims=True)
    d = o - mu
    var = jnp.mean(d * d, axis=-1, keepdims=True)
    o = d * lax.rsqrt(var + GN_EPS) * gng_ref[...] + gnb_ref[...]
    o_ref[...] = (rg_ref[...].astype(F32) * o).astype(BF16)


def retention(h, log_gamma, gn_g, gn_b, batch, seq, T=512):
    n = h.shape[0]
    H = RET_HEADS
    nc = seq // T
    W = RET_QK_DIM

    def col(base):
        return pl.BlockSpec((T, W), lambda b, hh, c: (b * nc + c, base // W + hh))

    vec = pl.BlockSpec((1, W), lambda b, hh, c: (0, hh))
    return pl.pallas_call(
        functools.partial(_ret_kernel, T=T),
        grid=(batch, H, nc),
        in_specs=[pl.BlockSpec(memory_space=pltpu.SMEM),
                  col(COL_RQ), col(COL_RK), col(COL_RV), col(COL_RG), vec, vec],
        out_specs=pl.BlockSpec((T, W), lambda b, hh, c: (b * nc + c, hh)),
        out_shape=jax.ShapeDtypeStruct((n, H * RET_V_DIM), BF16),
        scratch_shapes=[pltpu.VMEM((RET_QK_DIM, RET_V_DIM), F32), pltpu.VMEM((T, T), F32),
                        pltpu.VMEM((T, 1), F32), pltpu.VMEM((T, 1), F32)],
        compiler_params=_params("parallel", "parallel", "arbitrary"),
        name="retention",
    )(log_gamma, h, h, h, h, gn_g.reshape(1, -1), gn_b.reshape(1, -1))


def _out_ln_kernel(a_ref, r_ref, wa_ref, wr_ref, x_ref, g_ref, b_ref, o_ref, *, alpha):
    for s in range(a_ref.shape[0] // OUT_SUB):
        rows = slice(s * OUT_SUB, (s + 1) * OUT_SUB)
        mix = jnp.dot(a_ref[rows, :], wa_ref[...], preferred_element_type=F32)
        mix = mix + jnp.dot(r_ref[rows, :], wr_ref[...], preferred_element_type=F32)
        o_ref[rows, :] = _layer_norm(alpha * x_ref[rows, :] + mix, g_ref[...], b_ref[...])


def out_proj_ln(a, r, w, layer, x, g, b, alpha, bm=1024):
    n, d = x.shape
    ka, kr = a.shape[1], r.shape[1]
    assert ka == kr
    row = pl.BlockSpec((bm, d), lambda i: (i, 0))
    vec = pl.BlockSpec((1, d), lambda i: (0, 0))
    once = pl.Buffered(1)
    return pl.pallas_call(
        functools.partial(_out_ln_kernel, alpha=alpha),
        grid=(n // bm,),
        in_specs=[pl.BlockSpec((bm, ka), lambda i: (i, 0)),
                  pl.BlockSpec((bm, kr), lambda i: (i, 0)),
                  pl.BlockSpec((None, ka, d), lambda i: (layer, 0, 0), pipeline_mode=once),
                  pl.BlockSpec((None, kr, d), lambda i: (layer, 1, 0), pipeline_mode=once),
                  row, vec, vec],
        out_specs=row,
        out_shape=jax.ShapeDtypeStruct((n, d), F32),
        compiler_params=_params("parallel"),
        name="out_proj_ln",
    )(a, r, w, w, x, g.reshape(1, d), b.reshape(1, d))


def _ffn_ln_kernel(x_ref, wg_ref, wu_ref, wd_ref, g_ref, b_ref, o_ref, xb_ref, *, alpha):
    j = pl.program_id(1)
    last = pl.num_programs(1) - 1
    row_tiles = [slice(r, r + FFN_ROWS) for r in range(0, o_ref.shape[0], FFN_ROWS)]

    def hidden(xb):
        parts = []
        for s in range(wg_ref.shape[1] // FFN_SUB):
            cols = slice(s * FFN_SUB, (s + 1) * FFN_SUB)
            g = jnp.dot(xb, wg_ref[:, cols], preferred_element_type=F32)
            u = jnp.dot(xb, wu_ref[:, cols], preferred_element_type=F32)
            parts.append((g * jax.nn.sigmoid(g) * u).astype(BF16))
        return parts[0] if len(parts) == 1 else jnp.concatenate(parts, axis=1)

    @pl.when(j == 0)
    def _():
        for rows in row_tiles:
            xb = x_ref[rows, :].astype(BF16)
            xb_ref[rows, :] = xb
            o_ref[rows, :] = jnp.dot(hidden(xb), wd_ref[...], preferred_element_type=F32)

    @pl.when(jnp.logical_and(j > 0, j < last))
    def _():
        hm = hidden(xb_ref[...])
        for c in range(o_ref.shape[1] // FFN_OUT_CHUNK):
            cols = slice(c * FFN_OUT_CHUNK, (c + 1) * FFN_OUT_CHUNK)
            o_ref[:, cols] += jnp.dot(hm, wd_ref[:, cols], preferred_element_type=F32)

    @pl.when(j == last)
    def _():
        for rows in row_tiles:
            f = o_ref[rows, :] + jnp.dot(hidden(xb_ref[rows, :]), wd_ref[...], preferred_element_type=F32)
            o_ref[rows, :] = _layer_norm(alpha * x_ref[rows, :] + f, g_ref[...], b_ref[...])


def ffn_ln(x, wg, wu, wd, layer, g, b, alpha, bm=1024, bn=512):
    n, d = x.shape
    dff = wg.shape[2]
    row = pl.BlockSpec((bm, d), lambda i, j: (i, 0))
    vec = pl.BlockSpec((1, d), lambda i, j: (0, 0))
    wcol = pl.BlockSpec((None, d, bn), lambda i, j: (layer, 0, j))
    return pl.pallas_call(
        functools.partial(_ffn_ln_kernel, alpha=alpha),
        grid=(n // bm, dff // bn),
        in_specs=[row, wcol, wcol, pl.BlockSpec((None, bn, d), lambda i, j: (layer, j, 0)), vec, vec],
        out_specs=row,
        out_shape=jax.ShapeDtypeStruct((n, d), F32),
        scratch_shapes=[pltpu.VMEM((bm, d), BF16)],
        compiler_params=_params("parallel", "arbitrary"),
        name="ffn_ln",
    )(x, wg, wu, wd, g.reshape(1, d), b.reshape(1, d))


def _rotate_half_cols(w):
    half = w.shape[-1] // 2
    return jnp.concatenate([-w[..., half:], w[..., :half]], axis=-1)


def _prep_w_in(w):
    w_mla = w[..., :W_IN_MLA_COLS].astype(BF16)
    kr = w_mla[..., COL_KR:]
    pad = jnp.zeros(w.shape[:-1] + (MLA_IN_WIDTH - W_IN_MLA_COLS - MLA_ROPE_DIM,), BF16)
    w_mla = jnp.concatenate([w_mla, _rotate_half_cols(kr), pad], axis=-1)
    return w_mla, w[..., W_IN_MLA_COLS:].astype(BF16)


def _prep_w_uq(w):
    return w.astype(BF16).reshape(MLA_Q_LORA, MLA_HEADS, MLA_QK_DIM).transpose(1, 2, 0)


def _prep_w_ukv(w):
    w = w.astype(BF16).reshape(MLA_KV_LORA, MLA_HEADS, MLA_NOPE_DIM + MLA_V_DIM)
    wk = w[..., :MLA_NOPE_DIM].reshape(MLA_KV_LORA, MLA_HEADS * MLA_NOPE_DIM)
    return wk, w[..., MLA_NOPE_DIM:].transpose(1, 2, 0)


def _rope_tables(positions, dim):
    inv_freq = ROPE_THETA ** (-jnp.arange(0, dim, 2, dtype=F32) / dim)
    ang = positions.astype(F32)[..., None] * inv_freq
    return jnp.cos(ang), jnp.sin(ang)


def kernel(x, positions, ln_in_g, ln_in_b, w_in, q_norm_g, kv_norm_g, w_uq, w_ukv, ret_gn_g, ret_gn_b, w_out, ln1_g, ln1_b, w_gate, w_up, w_down, ln2_g, ln2_b):
    batch, seq, d = x.shape
    depth = w_in.shape[0]
    n = batch * seq
    alpha = (2 * depth) ** 0.25

    cos_m, sin_m = _rope_tables(positions, MLA_ROPE_DIM)
    cos_r, sin_r = _rope_tables(positions, RET_QK_DIM)
    cos2 = jnp.concatenate([cos_m, cos_m], axis=-1).reshape(n, MLA_ROPE_DIM)
    sin2 = jnp.concatenate([sin_m, sin_m], axis=-1).reshape(n, MLA_ROPE_DIM)
    cos_m_t, sin_m_t = cos_m.reshape(n, -1).T, sin_m.reshape(n, -1).T
    cos_r = cos_r.reshape(n, RET_QK_DIM // 2)
    sin_r = sin_r.reshape(n, RET_QK_DIM // 2)
    log_gamma = jnp.log1p(-jnp.exp2(-5.0 - jnp.arange(RET_HEADS, dtype=F32)))

    w_in_mla, w_in_ret = _prep_w_in(w_in)
    w_out_b = w_out.astype(BF16)
    w_gate_b, w_up_b, w_down_b = w_gate.astype(BF16), w_up.astype(BF16), w_down.astype(BF16)

    xf = ln_in(x.reshape(n, d), ln_in_g, ln_in_b)
    for l in range(depth):
        h_mla, h_ret = in_proj(xf, w_in_mla, w_in_ret, cos_r, sin_r, l)
        wk, wv_t = _prep_w_ukv(w_ukv[l])
        q_t, k, v_t = mla_proj(h_mla, cos2, sin2, cos_m_t, sin_m_t, q_norm_g[l].reshape(1, -1),
                               kv_norm_g[l].reshape(1, -1), _prep_w_uq(w_uq[l]), wk, wv_t, batch, seq)
        a = mla_attention(q_t, k, v_t)
        r = retention(h_ret, log_gamma, ret_gn_g[l], ret_gn_b[l], batch, seq)
        xf = out_proj_ln(a, r, w_out_b, l, xf, ln1_g[l], ln1_b[l], alpha)
        xf = ffn_ln(xf, w_gate_b, w_up_b, w_down_b, l, ln2_g[l], ln2_b[l], alpha)
    return xf.reshape(batch, seq, d)
```
